```python
import jax, jax.numpy as jnp
from jax import lax
import numpy as np

D_MODEL = 2048
BATCH = 4
SEQ = 2048
DEPTH = 4
DEC_BATCH = 8
DEC_SEQ = 4
PAST_LEN = 16384
PAGE_SIZE = 128

N_MIXERS = 2
N_LAYERS_A = (DEPTH + 1) // 2
N_LAYERS_B = DEPTH // 2
DIL_PATTERNS = ((128, 1), (512, 4), (2048, 16))
N_GROUPS_A = len(DIL_PATTERNS)
HEADS_A = 8
HEAD_DIM_A = 128
BLOCK_Q = 128
ROPE_THETA = 10000.0
HEADS_B = 8
DQK_B = D_MODEL // 2 // HEADS_B
DV_B = D_MODEL // HEADS_B
CHUNK_B = 64
GATE_CAP = 15.0
D_FF = 5632
CONV_W = 3
EPS = 1e-6

kernel_name = "dilated_attn_mlstm_convffn_step"


def rms_norm(x, g):
    xf = x.astype(jnp.float32)
    y = xf * lax.rsqrt(jnp.mean(xf * xf, axis=-1, keepdims=True) + EPS)
    return (y * g.astype(jnp.float32)).astype(x.dtype)


def rotary(x, pos):
    half = HEAD_DIM_A // 2
    inv_freq = ROPE_THETA ** (-jnp.arange(half, dtype=jnp.float32) / half)
    ang = pos.astype(jnp.float32)[:, None] * inv_freq[None, :]
    shape = (pos.shape[0],) + (1,) * (x.ndim - 3) + (half,)
    cos, sin = jnp.cos(ang).reshape(shape), jnp.sin(ang).reshape(shape)
    xf = x.astype(jnp.float32)
    x1, x2 = xf[..., :half], xf[..., half:]
    return jnp.concatenate([x1 * cos - x2 * sin, x2 * cos + x1 * sin], axis=-1).astype(x.dtype)


def attn_project(h, w_qkv, q_gain, k_gain, pos):
    B, S, _ = h.shape
    qkv = jnp.einsum('bsd,de->bse', h, w_qkv).reshape(B, S, N_GROUPS_A, 3, HEADS_A, HEAD_DIM_A)
    q = rotary(rms_norm(qkv[:, :, :, 0], q_gain), pos)
    k = rotary(rms_norm(qkv[:, :, :, 1], k_gain), pos)
    return q, k, qkv[:, :, :, 2]


def dilated_attn_prompt(q, k, v, dil, band):
    B, S, H, HD = q.shape
    L = S // dil
    n_blk = -(-L // BLOCK_Q)
    n_prev = -(-band // BLOCK_Q)
    Lp = n_blk * BLOCK_Q

    def classes(x):
        return x.reshape(B, L, dil, H, HD).transpose(0, 2, 1, 3, 4)

    qb = jnp.pad(classes(q), ((0, 0), (0, 0), (0, Lp - L), (0, 0), (0, 0))).reshape(B, dil, n_blk, BLOCK_Q, H, HD)

    def key_blocks(x):
        xp = jnp.pad(classes(x), ((0, 0), (0, 0), (n_prev * BLOCK_Q, Lp - L), (0, 0), (0, 0)))
        xp = xp.reshape(B, dil, n_blk + n_prev, BLOCK_Q, H, HD)
        return jnp.concatenate([xp[:, :, j:j + n_blk] for j in range(n_prev + 1)], axis=3)

    kb, vb = key_blocks(k), key_blocks(v)
    u_q = np.arange(Lp).reshape(n_blk, BLOCK_Q)
    u_k = (np.arange(n_blk)[:, None] - n_prev) * BLOCK_Q + np.arange((n_prev + 1) * BLOCK_Q)[None, :]
    delta = u_q[:, :, None] - u_k[:, None, :]
    mask = (delta >= 0) & (delta <= band) & (u_k[:, None, :] >= 0)
    s = jnp.einsum('brnqhd,brnkhd->brnhqk', qb, kb, preferred_element_type=jnp.float32) * (HEAD_DIM_A ** -0.5)
    s = jnp.where(mask[:, None], s, -jnp.inf)
    m = jnp.max(s, axis=-1, keepdims=True)
    p = jnp.exp(s - m)
    den = jnp.sum(p, axis=-1)
    o = jnp.einsum('brnhqk,brnkhd->brnqhd', p, vb.astype(jnp.float32)) / jnp.swapaxes(den, 3, 4)[..., None]
    lse = jnp.swapaxes(m[..., 0] + jnp.log(den), 3, 4)

    def positions(x):
        x = x.reshape((B, dil, Lp) + x.shape[4:])[:, :, :L]
        x = jnp.swapaxes(x, 1, 2)
        return x.reshape((B, S) + x.shape[3:])

    return positions(o), positions(lse)


def dilated_attn_sample(q, k_all, v_all, dil, band):
    T = q.shape[1]
    n_buf = k_all.shape[1] - T
    idx = n_buf + np.arange(T)[:, None] - dil * np.arange(band + 1)[None, :]
    valid = idx >= 0
    idx = np.maximum(idx, 0)
    kg, vg = k_all[:, idx], v_all[:, idx]
    s = jnp.einsum('bthd,btjhd->bthj', q, kg, preferred_element_type=jnp.float32) * (HEAD_DIM_A ** -0.5)
    s = jnp.where(valid[:, None, :], s, -jnp.inf)
    m = jnp.max(s, axis=-1, keepdims=True)
    p = jnp.exp(s - m)
    den = jnp.sum(p, axis=-1)
    o = jnp.einsum('bthj,btjhd->bthd', p, vg.astype(jnp.float32)) / den[..., None]
    return o, m[..., 0] + jnp.log(den)


def combine_groups(outs, lses, w_o, dtype):
    alpha = jax.nn.softmax(jnp.stack(lses, axis=-1), axis=-1)
    o = jnp.einsum('bshg,gbshd->bshd', alpha, jnp.stack(outs))
    B, S = o.shape[:2]
    return jnp.einsum('bse,ed->bsd', o.reshape(B, S, HEADS_A * HEAD_DIM_A).astype(dtype), w_o)


def attn_mixer_prompt(h, w_qkv, q_gain, k_gain, w_o):
    S = h.shape[1]
    q, k, v = attn_project(h, w_qkv, q_gain, k_gain, jnp.arange(S))
    outs, lses, bufs = [], [], []
    for g, (win, dil) in enumerate(DIL_PATTERNS):
        o_g, lse_g = dilated_attn_prompt(q[:, :, g], k[:, :, g], v[:, :, g], dil, win // dil)
        outs.append(o_g)
        lses.append(lse_g)
        n_keep = min(win, S)
        bufs.append(jnp.stack([k[:, S - n_keep:, g], v[:, S - n_keep:, g]], axis=2))
    return combine_groups(outs, lses, w_o, h.dtype), bufs


def attn_mixer_sample(h, bufs, w_qkv, q_gain, k_gain, w_o):
    T = h.shape[1]
    q, k, v = attn_project(h, w_qkv, q_gain, k_gain, PAST_LEN + jnp.arange(T))
    outs, lses, new_bufs = [], [], []
    for g, (win, dil) in enumerate(DIL_PATTERNS):
        n_buf = bufs[g].shape[1]
        kv_all = jnp.concatenate([bufs[g].astype(k.dtype), jnp.stack([k[:, :, g], v[:, :, g]], axis=2)], axis=1)
        o_g, lse_g = dilated_attn_sample(q[:, :, g], kv_all[:, :, 0], kv_all[:, :, 1], dil, win // dil)
        outs.append(o_g)
        lses.append(lse_g)
        new_bufs.append(kv_all[:, -n_buf:])
    return combine_groups(outs, lses, w_o, h.dtype), new_bufs


def mlstm_project(h, w_in, b_gates):
    B, S, _ = h.shape
    proj = jnp.einsum('bsd,de->bse', h, w_in)
    qk_w, v_w = HEADS_B * DQK_B, HEADS_B * DV_B
    q = proj[..., :qk_w].reshape(B, S, HEADS_B, DQK_B)
    k = proj[..., qk_w:2 * qk_w].reshape(B, S, HEADS_B, DQK_B) * (DQK_B ** -0.5)
    v = proj[..., 2 * qk_w:2 * qk_w + v_w].reshape(B, S, HEADS_B, DV_B)
    og = proj[..., 2 * qk_w + v_w:2 * qk_w + 2 * v_w]
    gates = proj[..., 2 * qk_w + 2 * v_w:].astype(jnp.float32) + b_gates.astype(jnp.float32)
    gates = GATE_CAP * jnp.tanh(gates / GATE_CAP)
    ig = jnp.swapaxes(gates[..., :HEADS_B], 1, 2)
    lf = jnp.swapaxes(jax.nn.log_sigmoid(gates[..., HEADS_B:]), 1, 2)

    def heads_first(x):
        return jnp.swapaxes(x, 1, 2).astype(jnp.float32)

    return heads_first(q), heads_first(k), heads_first(v), ig, lf, og


def mlstm_chunk(carry, inp):
    C, n, m = carry
    q, k, v, ig, lf = inp
    L = q.shape[2]
    b = jnp.cumsum(lf, axis=-1)
    causal = np.tril(np.ones((L, L), dtype=bool))
    Dm = jnp.where(causal, b[..., :, None] - b[..., None, :] + ig[..., None, :], -jnp.inf)
    inter = b + m[..., None]
    m_t = jnp.maximum(inter, jnp.max(Dm, axis=-1))
    w_inter = jnp.exp(inter - m_t)
    P = jnp.exp(Dm - m_t[..., None]) * jnp.einsum('bhtd,bhsd->bhts', q, k)
    num = w_inter[..., None] * jnp.einsum('bhed,bhtd->bhte', C, q) + jnp.einsum('bhts,bhse->bhte', P, v)
    den = w_inter * jnp.einsum('bhd,bhtd->bht', n, q) + jnp.sum(P, axis=-1)
    h = num / jnp.maximum(jnp.abs(den), jnp.exp(-m_t))[..., None]
    bL = b[..., -1]
    dec = bL[..., None] - b + ig
    m_new = jnp.maximum(bL + m, jnp.max(dec, axis=-1))
    wC = jnp.exp(bL + m - m_new)
    ws = jnp.exp(dec - m_new[..., None])
    C_new = wC[..., None, None] * C + jnp.einsum('bhs,bhse,bhsd->bhed', ws, v, k)
    n_new = wC[..., None] * n + jnp.einsum('bhs,bhsd->bhd', ws, k)
    return (C_new, n_new, m_new), h


def mlstm_output(hh, og, g_h, w_out, dtype):
    B, H, S, DV = hh.shape
    hh = jnp.swapaxes(hh, 1, 2)
    hn = hh * lax.rsqrt(jnp.mean(hh * hh, axis=-1, keepdims=True) + EPS)
    hn = hn.reshape(B, S, H * DV) * g_h.astype(jnp.float32) * jax.nn.sigmoid(og.astype(jnp.float32))
    return jnp.einsum('bse,ed->bsd', hn.astype(dtype), w_out)


def mlstm_mixer_prompt(h, w_in, b_gates, g_h, w_out):
    B, S, _ = h.shape
    q, k, v, ig, lf, og = mlstm_project(h, w_in, b_gates)
    n_chunks = S // CHUNK_B

    def chunks(x):
        return jnp.moveaxis(x.reshape(x.shape[:2] + (n_chunks, CHUNK_B) + x.shape[3:]), 2, 0)

    carry0 = (jnp.zeros((B, HEADS_B, DV_B, DQK_B), jnp.float32),
              jnp.zeros((B, HEADS_B, DQK_B), jnp.float32),
              jnp.zeros((B, HEADS_B), jnp.float32))
    (C, n, m), hs = lax.scan(mlstm_chunk, carry0, (chunks(q), chunks(k), chunks(v), chunks(ig), chunks(lf)))
    hh = jnp.moveaxis(hs, 0, 2).reshape(B, HEADS_B, S, DV_B)
    return mlstm_output(hh, og, g_h, w_out, h.dtype), C, n, m


def mlstm_mixer_sample(h, C0, n0, m0, w_in, b_gates, g_h, w_out):
    q, k, v, ig, lf, og = mlstm_project(h, w_in, b_gates)
    carry0 = (C0.astype(jnp.float32), n0.astype(jnp.float32), m0.astype(jnp.float32))
    (C, n, m), hh = mlstm_chunk(carry0, (q, k, v, ig, lf))
    return mlstm_output(hh, og, g_h, w_out, h.dtype), C, n, m


def conv_ffn(h, buf, w_up, conv_w, conv_b, w_down):
    T = h.shape[1]
    u = jnp.einsum('bsd,de->bse', h, w_up)
    ext = jnp.concatenate([buf.astype(u.dtype), u], axis=1)
    c = conv_b + conv_w[0] * ext[:, 0:T]
    for j in range(1, CONV_W):
        c = c + conv_w[j] * ext[:, j:j + T]
    a, b = jnp.split(c, 2, axis=-1)
    z = jax.nn.silu(a) * b
    return jnp.einsum('bsf,fd->bsd', z, w_down), ext[:, T:]


def setup_inputs(seed: int = 0) -> dict:
    key = jax.random.key(seed)
    ks = jax.random.split(key, 32)
    f32 = jnp.float32

    def normal(k, shape, scale=1.0):
        return scale * jax.random.normal(k, shape, f32)

    qkv_cols = N_GROUPS_A * 3 * HEADS_A * HEAD_DIM_A
    in_cols = 2 * HEADS_B * DQK_B + 2 * HEADS_B * DV_B + 2 * HEADS_B
    return {
        "x_prompt": normal(ks[0], (BATCH, SEQ, D_MODEL)),
        "x_sample": normal(ks[1], (DEC_BATCH, DEC_SEQ, D_MODEL)),
        "cache_kv_w128": normal(ks[2], (N_LAYERS_A, DEC_BATCH, min(DIL_PATTERNS[0][0], PAST_LEN), 2, HEADS_A, HEAD_DIM_A)),
        "cache_kv_w512": normal(ks[3], (N_LAYERS_A, DEC_BATCH, min(DIL_PATTERNS[1][0], PAST_LEN), 2, HEADS_A, HEAD_DIM_A)),
        "cache_kv_w2048": normal(ks[4], (N_LAYERS_A, DEC_BATCH, min(DIL_PATTERNS[2][0], PAST_LEN), 2, HEADS_A, HEAD_DIM_A)),
        "state_mlstm_C": normal(ks[5], (N_LAYERS_B, DEC_BATCH, HEADS_B, DV_B, DQK_B), 0.5),
        "state_mlstm_n": normal(ks[6], (N_LAYERS_B, DEC_BATCH, HEADS_B, DQK_B), 0.5),
        "state_mlstm_m": normal(ks[7], (N_LAYERS_B, DEC_BATCH, HEADS_B)),
        "state_ffn_conv": normal(ks[8], (DEPTH, DEC_BATCH, CONV_W - 1, 2 * D_FF)),
        "norm_mix": 1.0 + normal(ks[9], (DEPTH, D_MODEL), 0.02),
        "norm_ffn": 1.0 + normal(ks[10], (DEPTH, D_MODEL), 0.02),
        "attn_w_qkv": normal(ks[11], (N_LAYERS_A, D_MODEL, qkv_cols), D_MODEL ** -0.5),
        "attn_q_norm": 1.0 + normal(ks[12], (N_LAYERS_A, HEAD_DIM_A), 0.02),
        "attn_k_norm": 1.0 + normal(ks[13], (N_LAYERS_A, HEAD_DIM_A), 0.02),
        "attn_w_o": normal(ks[14], (N_LAYERS_A, HEADS_A * HEAD_DIM_A, D_MODEL), (HEADS_A * HEAD_DIM_A) ** -0.5),
        "mlstm_w_in": normal(ks[15], (N_LAYERS_B, D_MODEL, in_cols), D_MODEL ** -0.5),
        "mlstm_b_gates": jnp.concatenate([normal(ks[16], (N_LAYERS_B, HEADS_B), 0.1),
                                          3.0 + normal(ks[17], (N_LAYERS_B, HEADS_B), 0.5)], axis=-1),
        "mlstm_norm_h": 1.0 + normal(ks[18], (N_LAYERS_B, HEADS_B * DV_B), 0.02),
        "mlstm_w_out": normal(ks[19], (N_LAYERS_B, HEADS_B * DV_B, D_MODEL), (HEADS_B * DV_B) ** -0.5),
        "ffn_w_up": normal(ks[20], (DEPTH, D_MODEL, 2 * D_FF), D_MODEL ** -0.5),
        "ffn_conv_w": normal(ks[21], (DEPTH, CONV_W, 2 * D_FF), CONV_W ** -0.5),
        "ffn_conv_b": normal(ks[22], (DEPTH, 2 * D_FF), 0.02),
        "ffn_w_down": normal(ks[23], (DEPTH, D_FF, D_MODEL), D_FF ** -0.5),
    }


def reference(x_prompt, x_sample, cache_kv_w128, cache_kv_w512, cache_kv_w2048,
              state_mlstm_C, state_mlstm_n, state_mlstm_m, state_ffn_conv,
              norm_mix, norm_ffn, attn_w_qkv, attn_q_norm, attn_k_norm, attn_w_o,
              mlstm_w_in, mlstm_b_gates, mlstm_norm_h, mlstm_w_out,
              ffn_w_up, ffn_conv_w, ffn_conv_b, ffn_w_down):
    cache_kv = (cache_kv_w128, cache_kv_w512, cache_kv_w2048)
    xp, xs = x_prompt, x_sample
    kv_p = [[] for _ in DIL_PATTERNS]
    kv_s = [[] for _ in DIL_PATTERNS]
    C_p, n_p, m_p, C_s, n_s, m_s = [], [], [], [], [], []
    conv_p, conv_s = [], []
    for layer in range(DEPTH):
        hp = rms_norm(xp, norm_mix[layer])
        hs = rms_norm(xs, norm_mix[layer])
        if layer % N_MIXERS == 0:
            ia = layer // N_MIXERS
            yp, bufs_p = attn_mixer_prompt(hp, attn_w_qkv[ia], attn_q_norm[ia], attn_k_norm[ia], attn_w_o[ia])
            ys, bufs_s = attn_mixer_sample(hs, [c[ia] for c in cache_kv], attn_w_qkv[ia],
                                           attn_q_norm[ia], attn_k_norm[ia], attn_w_o[ia])
            for g in range(N_GROUPS_A):
                kv_p[g].append(bufs_p[g])
                kv_s[g].append(bufs_s[g])
        else:
            ib = layer // N_MIXERS
            yp, Cp, np_, mp = mlstm_mixer_prompt(hp, mlstm_w_in[ib], mlstm_b_gates[ib], mlstm_norm_h[ib], mlstm_w_out[ib])
            ys, Cs, ns, ms = mlstm_mixer_sample(hs, state_mlstm_C[ib], state_mlstm_n[ib], state_mlstm_m[ib],
                                                mlstm_w_in[ib], mlstm_b_gates[ib], mlstm_norm_h[ib], mlstm_w_out[ib])
            C_p.append(Cp); n_p.append(np_); m_p.append(mp)
            C_s.append(Cs); n_s.append(ns); m_s.append(ms)
        xp = xp + yp
        xs = xs + ys
        hp = rms_norm(xp, norm_ffn[layer])
        hs = rms_norm(xs, norm_ffn[layer])
        zero_buf = jnp.zeros((xp.shape[0], CONV_W - 1, 2 * D_FF), xp.dtype)
        yp, cp = conv_ffn(hp, zero_buf, ffn_w_up[layer], ffn_conv_w[layer], ffn_conv_b[layer], ffn_w_down[layer])
        ys, cs = conv_ffn(hs, state_ffn_conv[layer], ffn_w_up[layer], ffn_conv_w[layer], ffn_conv_b[layer], ffn_w_down[layer])
        conv_p.append(cp)
        conv_s.append(cs)
        xp = xp + yp
        xs = xs + ys
    dt = x_prompt.dtype
    kv_w128_prompt, kv_w512_prompt, kv_w2048_prompt = [jnp.stack(b) for b in kv_p]
    kv_w128_sample, kv_w512_sample, kv_w2048_sample = [jnp.stack(b) for b in kv_s]
    mlstm_C_prompt = jnp.stack(C_p).astype(dt)
    mlstm_n_prompt = jnp.stack(n_p).astype(dt)
    mlstm_m_prompt = jnp.stack(m_p).astype(dt)
    mlstm_C_sample = jnp.stack(C_s).astype(dt)
    mlstm_n_sample = jnp.stack(n_s).astype(dt)
    mlstm_m_sample = jnp.stack(m_s).astype(dt)
    ffn_conv_prompt = jnp.stack(conv_p)
    ffn_conv_sample = jnp.stack(conv_s)
    return (xp, xs, kv_w128_prompt, kv_w128_sample, kv_w512_prompt, kv_w512_sample,
            kv_w2048_prompt, kv_w2048_sample, mlstm_C_prompt, mlstm_C_sample,
            mlstm_n_prompt, mlstm_n_sample, mlstm_m_prompt, mlstm_m_sample,
            ffn_conv_prompt, ffn_conv_sample)
```

```python
import functools

import numpy as np
import jax
import jax.numpy as jnp
from jax import lax
from jax.experimental import pallas as pl
from jax.experimental.pallas import tpu as pltpu

F32 = jnp.float32
BF16 = jnp.bfloat16

DIL_PATTERNS = ((128, 1), (512, 4), (2048, 16))
N_GROUPS = len(DIL_PATTERNS)
HEADS_A = 8
HEAD_DIM = 128
ATT_W = HEADS_A * HEAD_DIM
BAND = 128
ROPE_THETA = 10000.0
PAST_LEN = 16384
HEADS_B = 8
GATE_CAP = 15.0
CONV_W = 3
EPS = 1e-6
NEG = -1e30

LANES = 128
SUBLANES = 8
VMEM_LIMIT_BYTES = 60 * 1024 * 1024
ROW_TILE = 1024
PROJ_COL_TILE = 512
FFN_COL_TILE = 256
MLSTM_CHUNK = 256
SAMPLE_PAD_T = 16


def _cparams(sem):
    return pltpu.CompilerParams(dimension_semantics=sem, vmem_limit_bytes=VMEM_LIMIT_BYTES)


def _rms_rows(x, g):
    ms = jnp.mean(x * x, axis=-1, keepdims=True)
    return (x * lax.rsqrt(ms + EPS)) * g


def _log_sigmoid(x):
    return jnp.minimum(x, 0.0) - jnp.log1p(jnp.exp(-jnp.abs(x)))


def _dot(a, b):
    return jnp.dot(a, b, preferred_element_type=F32)


def _dot_nt(a, b):
    return lax.dot_general(a, b, (((1,), (1,)), ((), ())), preferred_element_type=F32)


def _dot_tn(a, b):
    return lax.dot_general(a, b, (((0,), (0,)), ((), ())), preferred_element_type=F32)


def _qkv_proj_kernel(x_ref, g_ref, w_ref, gains_ref, cos_ref, sin_ref,
                     qkv_ref, kv0_ref, kv1_ref, kv2_ref, h_scr, y_scr, *, tiles_per_slab):
    j = pl.program_id(1)
    slab = j // tiles_per_slab
    kind = slab % 3

    @pl.when(j == 0)
    def _():
        h_scr[...] = _rms_rows(x_ref[...], g_ref[...]).astype(BF16)

    y_scr[...] = _dot(h_scr[...], w_ref[...].astype(BF16))

    @pl.when(kind != 2)
    def _():
        gain = gains_ref[pl.ds(kind, 1), :]
        cos = cos_ref[...]
        sin = sin_ref[...]
        for hh in range(y_scr.shape[1] // HEAD_DIM):
            cols = slice(hh * HEAD_DIM, (hh + 1) * HEAD_DIM)
            yn = _rms_rows(y_scr[:, cols], gain)
            y_scr[:, cols] = yn * cos + pltpu.roll(yn, HEAD_DIM // 2, 1) * sin

    qkv_ref[...] = y_scr[...].astype(BF16)
    for g, kv_ref in enumerate((kv0_ref, kv1_ref, kv2_ref)):
        @pl.when((slab == 3 * g + 1) | (slab == 3 * g + 2))
        def _(kv_ref=kv_ref):
            kv_ref[...] = y_scr[...]


def _qkv_proj(x, g, w, gains, cos, sin, tm):
    m, d = x.shape
    n = w.shape[1]
    tn = PROJ_COL_TILE
    tps = ATT_W // tn
    nj = n // tn
    n_pos_tiles = cos.shape[0] // tm
    kv_blocks = 2 * tps

    def kv_map(g):
        first = (3 * g + 1) * tps
        return lambda i, j: (i, jnp.clip(j - first, 0, kv_blocks - 1))

    kernel = functools.partial(_qkv_proj_kernel, tiles_per_slab=tps)
    return pl.pallas_call(
        kernel,
        grid=(m // tm, nj),
        in_specs=[
            pl.BlockSpec((tm, d), lambda i, j: (i, 0)),
            pl.BlockSpec((1, d), lambda i, j: (0, 0)),
            pl.BlockSpec((d, tn), lambda i, j: (0, j)),
            pl.BlockSpec((2, HEAD_DIM), lambda i, j: (0, 0)),
            pl.BlockSpec((tm, HEAD_DIM), lambda i, j: (i % n_pos_tiles, 0)),
            pl.BlockSpec((tm, HEAD_DIM), lambda i, j: (i % n_pos_tiles, 0)),
        ],
        out_specs=[
            pl.BlockSpec((tm, tn), lambda i, j: (i, j)),
            pl.BlockSpec((tm, tn), kv_map(0)),
            pl.BlockSpec((tm, tn), kv_map(1)),
            pl.BlockSpec((tm, tn), kv_map(2)),
        ],
        out_shape=[
            jax.ShapeDtypeStruct((m, n), BF16),
            jax.ShapeDtypeStruct((m, 2 * ATT_W), F32),
            jax.ShapeDtypeStruct((m, 2 * ATT_W), F32),
            jax.ShapeDtypeStruct((m, 2 * ATT_W), F32),
        ],
        scratch_shapes=[pltpu.VMEM((tm, d), BF16), pltpu.VMEM((tm, tn), F32)],
        compiler_params=_cparams(("arbitrary", "arbitrary")),
        name="attn_qkv_proj",
    )(x, g, w, gains, cos, sin)


def _attn_prompt_kernel(*refs, tq, n_prev_groups):
    q_ref, k_ref, v_ref, kp_ref, vp_ref = refs[:5]
    prev = refs[5:5 + 2 * n_prev_groups]
    outs = refs[5 + 2 * n_prev_groups:]
    if n_prev_groups:
        (comb_ref, kbuf, vbuf) = outs
    else:
        (o_ref, lse_ref, kbuf, vbuf) = outs
    n = pl.program_id(2)
    scale = HEAD_DIM ** -0.5

    kbuf[0:BAND, :] = kp_ref[0]
    vbuf[0:BAND, :] = vp_ref[0]
    kbuf[BAND:BAND + tq, :] = k_ref[0]
    vbuf[BAND:BAND + tq, :] = v_ref[0]

    qi = lax.broadcasted_iota(jnp.int32, (BAND, 2 * BAND), 0)
    kj = lax.broadcasted_iota(jnp.int32, (BAND, 2 * BAND), 1)
    bias = jnp.where((kj >= qi) & (kj <= qi + BAND), 0.0, NEG).astype(F32)
    first_pen = jnp.where(n == 0, NEG, 0.0).astype(F32)
    bias_first = bias + jnp.where(kj < BAND, first_pen, 0.0)
    lane = lax.broadcasted_iota(jnp.int32, (BAND, LANES), 1)

    for c in range(tq // BAND):
        rows = slice(c * BAND, (c + 1) * BAND)
        krows = slice(c * BAND, c * BAND + 2 * BAND)
        b = bias_first if c == 0 else bias
        lse_tile = jnp.zeros((BAND, LANES), F32)
        for h in range(HEADS_A):
            cols = slice(h * HEAD_DIM, (h + 1) * HEAD_DIM)
            s = _dot_nt(q_ref[0, rows, cols], kbuf[krows, cols]) * scale + b
            mx = jnp.max(s, axis=-1, keepdims=True)
            p = jnp.exp(s - mx)
            den = jnp.sum(p, axis=-1, keepdims=True)
            o = _dot(p.astype(BF16), vbuf[krows, cols]) / den
            lse = mx + jnp.log(den)
            if n_prev_groups:
                ls = [prev[2 * g + 1][0, rows, h:h + 1] for g in range(n_prev_groups)] + [lse]
                os_ = [prev[2 * g][0, rows, cols] for g in range(n_prev_groups)] + [o]
                top = functools.reduce(jnp.maximum, ls)
                es = [jnp.exp(l - top) for l in ls]
                tot = functools.reduce(lambda a, b_: a + b_, es)
                acc = functools.reduce(lambda a, b_: a + b_, [e * o_ for e, o_ in zip(es, os_)])
                comb_ref[0, rows, cols] = (acc / tot).astype(BF16)
            else:
                o_ref[0, rows, cols] = o
                lse_tile = jnp.where(lane == h, lse, lse_tile)
        if not n_prev_groups:
            lse_ref[0, rows, :] = lse_tile


def _attn_prompt(qkv, g, batch, seq, prev):
    win, dil = DIL_PATTERNS[g]
    l = seq // dil
    tq = min(l, 512)
    nq = l // tq
    ncol = qkv.shape[1]
    blocks_per_pos = ncol // ATT_W
    qv = qkv.reshape(batch, l, dil * ncol)
    sub = tq // BAND

    def col(kind):
        return lambda b, r, n: (b, n, r * blocks_per_pos + 3 * g + kind)

    def col_prev(kind):
        return lambda b, r, n: (b, jnp.maximum(n * sub - 1, 0), r * blocks_per_pos + 3 * g + kind)

    o_spec = pl.BlockSpec((1, tq, ATT_W), lambda b, r, n: (b, n, r))
    lse_spec = pl.BlockSpec((1, tq, LANES), lambda b, r, n: (b, n, r))
    in_specs = [
        pl.BlockSpec((1, tq, ATT_W), col(0)),
        pl.BlockSpec((1, tq, ATT_W), col(1)),
        pl.BlockSpec((1, tq, ATT_W), col(2)),
        pl.BlockSpec((1, BAND, ATT_W), col_prev(1)),
        pl.BlockSpec((1, BAND, ATT_W), col_prev(2)),
    ]
    args = [qv, qv, qv, qv, qv]
    for o_prev, lse_prev in prev:
        in_specs += [o_spec, lse_spec]
        args += [o_prev, lse_prev]
    if prev:
        out_specs = [o_spec]
        out_shape = [jax.ShapeDtypeStruct((batch, l, dil * ATT_W), BF16)]
    else:
        out_specs = [o_spec, lse_spec]
        out_shape = [jax.ShapeDtypeStruct((batch, l, dil * ATT_W), F32),
                     jax.ShapeDtypeStruct((batch, l, dil * LANES), F32)]
    kernel = functools.partial(_attn_prompt_kernel, tq=tq, n_prev_groups=len(prev))
    return pl.pallas_call(
        kernel,
        grid=(batch, dil, nq),
        in_specs=in_specs,
        out_specs=out_specs,
        out_shape=out_shape,
        scratch_shapes=[pltpu.VMEM((BAND + tq, ATT_W), BF16), pltpu.VMEM((BAND + tq, ATT_W), BF16)],
        compiler_params=_cparams(("arbitrary", "arbitrary", "arbitrary")),
        name=f"attn_prompt_g{g}",
    )(*args)


def _attn_sample_kernel(qkv_ref, c0_ref, c1_ref, c2_ref, o_ref, *, t_new):
    scale = HEAD_DIM ** -0.5
    caches = (c0_ref, c1_ref, c2_ref)
    head_row = lax.broadcasted_iota(jnp.int32, (HEADS_A, ATT_W), 0)
    head_of_lane = lax.broadcasted_iota(jnp.int32, (HEADS_A, ATT_W), 1) // HEAD_DIM
    diag = head_row == head_of_lane
    key_u = lax.broadcasted_iota(jnp.int32, (HEADS_A, BAND), 1)
    qkv = qkv_ref[0].astype(F32)

    out_rows = []
    for t in range(t_new):
        o_g, lse_g = [], []
        for g, (_, dil) in enumerate(DIL_PATTERNS):
            base = 3 * g * ATT_W
            q_t = qkv[t:t + 1, base:base + ATT_W]
            q_exp = jnp.where(diag, q_t, 0.0)
            off = 0 if dil == 1 else t * 2 * ATT_W
            k_c = caches[g][0, :, off:off + ATT_W].astype(BF16)
            v_c = caches[g][0, :, off + ATT_W:off + 2 * ATT_W].astype(BF16)
            s_c = _dot_nt(q_exp.astype(BF16), k_c) * scale
            if dil == 1:
                s_c = jnp.where(key_u >= t, s_c, NEG)
                new_ts = list(range(t + 1))
            else:
                new_ts = [t]
            s_n = []
            for t2 in new_ts:
                k_n = qkv[t2:t2 + 1, base + ATT_W:base + 2 * ATT_W]
                s_n.append(jnp.sum(q_exp * k_n, axis=-1, keepdims=True) * scale)
            mx = jnp.max(s_c, axis=-1, keepdims=True)
            for s in s_n:
                mx = jnp.maximum(mx, s)
            p_c = jnp.exp(s_c - mx)
            den = jnp.sum(p_c, axis=-1, keepdims=True)
            acc = _dot(p_c.astype(BF16), v_c)
            for t2, s in zip(new_ts, s_n):
                p_n = jnp.exp(s - mx)
                den = den + p_n
                acc = acc + p_n * qkv[t2:t2 + 1, base + 2 * ATT_W:base + 3 * ATT_W]
            o_g.append(acc / den)
            lse_g.append(mx + jnp.log(den))
        top = functools.reduce(jnp.maximum, lse_g)
        es = [jnp.exp(l - top) for l in lse_g]
        tot = functools.reduce(lambda a, b_: a + b_, es)
        comb = functools.reduce(lambda a, b_: a + b_, [e * o_ for e, o_ in zip(es, o_g)]) / tot
        out_rows.append(jnp.sum(jnp.where(diag, comb, 0.0), axis=0, keepdims=True))
    o_ref[0] = jnp.concatenate(out_rows, axis=0)


def _attn_sample(qkv, caches, batch, t_new):
    ncol = qkv.shape[1]
    in_specs = [pl.BlockSpec((1, t_new, ncol), lambda b: (b, 0, 0))]
    args = [qkv.reshape(batch, t_new, ncol)]
    for (win, dil), c in zip(DIL_PATTERNS, caches):
        width = min(dil, t_new) * 2 * ATT_W
        in_specs.append(pl.BlockSpec((1, BAND, width), lambda b: (b, 0, 0)))
        args.append(c.reshape(batch, BAND, dil * 2 * ATT_W))
    kernel = functools.partial(_attn_sample_kernel, t_new=t_new)
    out = pl.pallas_call(
        kernel,
        grid=(batch,),
        in_specs=in_specs,
        out_specs=pl.BlockSpec((1, t_new, ATT_W), lambda b: (b, 0, 0)),
        out_shape=jax.ShapeDtypeStruct((batch, t_new, ATT_W), F32),
        compiler_params=_cparams(("arbitrary",)),
        name="attn_sample",
    )(*args)
    return out.reshape(batch * t_new, ATT_W)


def _matmul_residual_kernel(a_ref, w_ref, x_ref, o_ref):
    o_ref[...] = x_ref[...] + _dot(a_ref[...].astype(BF16), w_ref[...].astype(BF16))


def _matmul_residual(a, w, x, tm):
    m, k = a.shape
    n = w.shape[1]
    tn = 1024
    return pl.pallas_call(
        _matmul_residual_kernel,
        grid=(m // tm, n // tn),
        in_specs=[
            pl.BlockSpec((tm, k), lambda i, j: (i, 0)),
            pl.BlockSpec((k, tn), lambda i, j: (0, j)),
            pl.BlockSpec((tm, tn), lambda i, j: (i, j)),
        ],
        out_specs=pl.BlockSpec((tm, tn), lambda i, j: (i, j)),
        out_shape=jax.ShapeDtypeStruct((m, n), F32),
        compiler_params=_cparams(("arbitrary", "arbitrary")),
        name="matmul_residual",
    )(a, w, x)


def _gate_act(pre, is_input_gate):
    capped = GATE_CAP * jnp.tanh(pre / GATE_CAP)
    return jnp.where(is_input_gate, capped, _log_sigmoid(capped))


def _mlstm_in_kernel(x_ref, g_ref, w_ref, wg_ref, wgt_ref, bg_ref, bgt_ref,
                     qkv_ref, og_ref, gates_ref, gatest_ref, h_scr, *, k_tiles, n_qkv_tiles, k_scale):
    j = pl.program_id(1)

    @pl.when(j == 0)
    def _():
        h = _rms_rows(x_ref[...], g_ref[...]).astype(BF16)
        h_scr[...] = h
        n_gates = wgt_ref.shape[0]
        pre = _dot(h, wg_ref[...].astype(BF16)) + bg_ref[...]
        lane = lax.broadcasted_iota(jnp.int32, pre.shape, 1)
        gates_ref[...] = jnp.where(lane < n_gates, _gate_act(pre, lane < n_gates // 2), 0.0)
        pre_t = _dot_nt(wgt_ref[...].astype(BF16), h) + bgt_ref[...]
        row = lax.broadcasted_iota(jnp.int32, pre_t.shape, 0)
        gatest_ref[...] = _gate_act(pre_t, row < n_gates // 2)

    y = _dot(h_scr[...], w_ref[...].astype(BF16))

    @pl.when(j < n_qkv_tiles)
    def _():
        is_k = (j >= k_tiles[0]) & (j < k_tiles[1])
        qkv_ref[...] = (y * jnp.where(is_k, k_scale, 1.0).astype(F32)).astype(BF16)

    @pl.when(j >= n_qkv_tiles)
    def _():
        og_ref[...] = y


def _mlstm_in_proj(x, g, w, wgt, bg, bgt, tm, dqk, dv):
    m, d = x.shape
    tn = PROJ_COL_TILE
    qk_w, v_w = HEADS_B * dqk, HEADS_B * dv
    n_main = 2 * qk_w + 2 * v_w
    n_gates = w.shape[1] - n_main
    n_qkv_tiles = (2 * qk_w + v_w) // tn
    nj = n_main // tn
    kernel = functools.partial(_mlstm_in_kernel, k_tiles=(qk_w // tn, 2 * qk_w // tn),
                               n_qkv_tiles=n_qkv_tiles, k_scale=dqk ** -0.5)
    return pl.pallas_call(
        kernel,
        grid=(m // tm, nj),
        in_specs=[
            pl.BlockSpec((tm, d), lambda i, j: (i, 0)),
            pl.BlockSpec((1, d), lambda i, j: (0, 0)),
            pl.BlockSpec((d, tn), lambda i, j: (0, j)),
            pl.BlockSpec((d, LANES), lambda i, j: (0, n_main // LANES)),
            pl.BlockSpec((n_gates, d), lambda i, j: (0, 0)),
            pl.BlockSpec((1, LANES), lambda i, j: (0, 0)),
            pl.BlockSpec((n_gates, 1), lambda i, j: (0, 0)),
        ],
        out_specs=[
            pl.BlockSpec((tm, tn), lambda i, j: (i, jnp.minimum(j, n_qkv_tiles - 1))),
            pl.BlockSpec((tm, tn), lambda i, j: (i, jnp.maximum(j - n_qkv_tiles, 0))),
            pl.BlockSpec((tm, LANES), lambda i, j: (i, 0)),
            pl.BlockSpec((n_gates, tm), lambda i, j: (0, i)),
        ],
        out_shape=[
            jax.ShapeDtypeStruct((m, 2 * qk_w + v_w), BF16),
            jax.ShapeDtypeStruct((m, v_w), F32),
            jax.ShapeDtypeStruct((m, LANES), F32),
            jax.ShapeDtypeStruct((n_gates, m), F32),
        ],
        scratch_shapes=[pltpu.VMEM((tm, d), BF16)],
        compiler_params=_cparams(("arbitrary", "arbitrary")),
        name="mlstm_in_proj",
    )(x, g, w, w, wgt, bg, bgt)


def _split3(x):
    hi = x.astype(BF16)
    r = x - hi.astype(F32)
    mid = r.astype(BF16)
    lo = (r - mid.astype(F32)).astype(BF16)
    return hi, mid, lo


def _mlstm_core_kernel(*refs, chunk, dqk, dv, has_state):
    q_ref, k_ref, v_ref, og_ref, gates_ref, gatest_ref, gh_ref = refs[:7]
    rest = refs[7:]
    if has_state:
        c0_ref, n0_ref, m0_ref = rest[:3]
        rest = rest[3:]
    out_ref, c_out, n_out, m_out, c_scr, n_scr, m_scr = rest
    c = pl.program_id(1)
    n_heads = c_scr.shape[0]

    @pl.when(c == 0)
    def _():
        if has_state:
            c_scr[...] = c0_ref[0]
            n_scr[...] = n0_ref[0]
            m_scr[...] = m0_ref[0]
        else:
            c_scr[...] = jnp.zeros_like(c_scr)
            n_scr[...] = jnp.zeros_like(n_scr)
            m_scr[...] = jnp.zeros_like(m_scr)

    row = lax.broadcasted_iota(jnp.int32, (chunk, chunk), 0)
    col = lax.broadcasted_iota(jnp.int32, (chunk, chunk), 1)
    causal = col <= row
    tri_lo = jnp.where(causal, 1.0, 0.0).astype(BF16)
    tri_up = jnp.where(row <= col, 1.0, 0.0).astype(BF16)

    gates = gates_ref[0]
    gates_t = gatest_ref[0] if len(gatest_ref.shape) == 3 else gatest_ref[...]
    b_cols = functools.reduce(lambda a, b_: a + b_, [_dot(tri_lo, p) for p in _split3(gates)])
    b_rows = functools.reduce(lambda a, b_: a + b_, [_dot(p, tri_up) for p in _split3(gates_t)])

    for hh in range(n_heads):
        q = q_ref[0, :, hh * dqk:(hh + 1) * dqk]
        k = k_ref[0, :, hh * dqk:(hh + 1) * dqk]
        v = v_ref[0, :, hh * dv:(hh + 1) * dv]
        ig_col = gates[:, hh:hh + 1]
        b_col = b_cols[:, n_heads + hh:n_heads + hh + 1]
        ig_row = gates_t[hh:hh + 1, :]
        b_row = b_rows[n_heads + hh:n_heads + hh + 1, :]
        m_prev = m_scr[hh][:, 0:1]
        c_state = c_scr[hh]
        n_state = n_scr[hh]

        dm = jnp.where(causal, b_col + (ig_row - b_row), NEG)
        inter = b_col + m_prev
        m_t = jnp.maximum(inter, jnp.max(dm, axis=-1, keepdims=True))
        w_inter = jnp.exp(inter - m_t)
        p = jnp.exp(dm - m_t) * _dot_nt(q, k)
        num = w_inter * _dot_nt(q, c_state.astype(BF16)) + _dot(p.astype(BF16), v)
        nq = jnp.sum(q.astype(F32) * n_state, axis=-1, keepdims=True)
        den = w_inter * nq + jnp.sum(p, axis=-1, keepdims=True)
        hid = num / jnp.maximum(jnp.abs(den), jnp.exp(-m_t))
        hn = hid * lax.rsqrt(jnp.mean(hid * hid, axis=-1, keepdims=True) + EPS)
        cols = slice(hh * dv, (hh + 1) * dv)
        gated = hn * gh_ref[:, cols] * jax.nn.sigmoid(og_ref[0, :, cols])
        out_ref[0, :, cols] = gated.astype(BF16)

        b_last = b_col[chunk - 1:chunk, :]
        dec = b_last - b_col + ig_col
        m_new = jnp.maximum(b_last + m_prev, jnp.max(dec, axis=0, keepdims=True))
        w_c = jnp.exp(b_last + m_prev - m_new)
        w_s = jnp.exp(dec - m_new)
        vw = (v.astype(F32) * w_s).astype(BF16)
        c_scr[hh] = w_c * c_state + _dot_tn(vw, k)
        n_scr[hh] = w_c * n_state + jnp.sum(k.astype(F32) * w_s, axis=0, keepdims=True)
        m_scr[hh] = jnp.broadcast_to(m_new, (1, LANES))

    @pl.when(c == pl.num_programs(1) - 1)
    def _():
        c_out[0] = c_scr[...]
        n_out[0] = n_scr[...]
        m_out[0] = m_scr[...]


def _mlstm_core(qkv, og, gates, gates_t, gh, batch, seq, chunk, dqk, dv, state):
    n_heads = HEADS_B
    qk_w, v_w = n_heads * dqk, n_heads * dv
    nc = seq // chunk
    qkv3 = qkv.reshape(batch, seq, 2 * qk_w + v_w)
    state_specs = [
        pl.BlockSpec((1, n_heads, dv, dqk), lambda b, c: (b, 0, 0, 0)),
        pl.BlockSpec((1, n_heads, 1, dqk), lambda b, c: (b, 0, 0, 0)),
        pl.BlockSpec((1, n_heads, 1, LANES), lambda b, c: (b, 0, 0, 0)),
    ]
    in_specs = [
        pl.BlockSpec((1, chunk, qk_w), lambda b, c: (b, c, 0)),
        pl.BlockSpec((1, chunk, qk_w), lambda b, c: (b, c, 1)),
        pl.BlockSpec((1, chunk, v_w), lambda b, c: (b, c, 2 * qk_w // v_w)),
        pl.BlockSpec((1, chunk, v_w), lambda b, c: (b, c, 0)),
        pl.BlockSpec((1, chunk, LANES), lambda b, c: (b, c, 0)),
        pl.BlockSpec((2 * n_heads, chunk), lambda b, c: (0, b * nc + c)) if gates_t.ndim == 2
        else pl.BlockSpec((1, 2 * n_heads, chunk), lambda b, c: (b * nc + c, 0, 0)),
        pl.BlockSpec((1, v_w), lambda b, c: (0, 0)),
    ]
    args = [qkv3, qkv3, qkv3, og.reshape(batch, seq, v_w), gates.reshape(batch, seq, LANES), gates_t, gh]
    if state is not None:
        in_specs += state_specs
        args += list(state)
    kernel = functools.partial(_mlstm_core_kernel, chunk=chunk, dqk=dqk, dv=dv,
                               has_state=state is not None)
    return pl.pallas_call(
        kernel,
        grid=(batch, nc),
        in_specs=in_specs,
        out_specs=[pl.BlockSpec((1, chunk, v_w), lambda b, c: (b, c, 0))] + state_specs,
        out_shape=[
            jax.ShapeDtypeStruct((batch, seq, v_w), BF16),
            jax.ShapeDtypeStruct((batch, n_heads, dv, dqk), F32),
            jax.ShapeDtypeStruct((batch, n_heads, 1, dqk), F32),
            jax.ShapeDtypeStruct((batch, n_heads, 1, LANES), F32),
        ],
        scratch_shapes=[pltpu.VMEM((n_heads, dv, dqk), F32), pltpu.VMEM((n_heads, 1, dqk), F32),
                        pltpu.VMEM((n_heads, 1, LANES), F32)],
        compiler_params=_cparams(("arbitrary", "arbitrary")),
        name="mlstm_core",
    )(*args)


def _ffn_kernel(*refs, tm, tiles_per_seq, t_new):
    x_ref, g_ref, wa_ref, wb_ref, cwa_ref, cwb_ref, cba_ref, cbb_ref, wd_ref = refs[:9]
    rest = refs[9:]
    sample = t_new is not None
    if sample:
        s1_ref, s2_ref, o_ref, u_ref, h_scr, ext_scr = rest
    else:
        o_ref, tail_ref, h_scr, ext_scr, tail_scr = rest
    i = pl.program_id(0)
    f = pl.program_id(1)
    pad = SUBLANES

    @pl.when(f == 0)
    def _():
        x = x_ref[...]
        h_scr[...] = _rms_rows(x, g_ref[...]).astype(BF16)
        o_ref[...] = x

    h = h_scr[...]
    conv = []
    for part, (w_ref, cw_ref, cb_ref) in enumerate(((wa_ref, cwa_ref, cba_ref), (wb_ref, cwb_ref, cbb_ref))):
        u = _dot(h, w_ref[...].astype(BF16))
        ext_scr[part, pad:pad + tm, :] = u
        if sample:
            ext_scr[part, 0:pad, :] = jnp.zeros((pad, u.shape[1]), F32)
            u_ref[part] = u
            r = lax.broadcasted_iota(jnp.int32, u.shape, 0) % t_new
            u1 = jnp.where(r >= 1, ext_scr[part, pad - 1:pad - 1 + tm, :], s1_ref[part])
            u2 = jnp.where(r >= 2, ext_scr[part, pad - 2:pad - 2 + tm, :], s2_ref[part])
        else:
            @pl.when(i % tiles_per_seq == 0)
            def _(part=part, u=u):
                ext_scr[part, 0:pad, :] = jnp.zeros((pad, u.shape[1]), F32)

            @pl.when(i % tiles_per_seq != 0)
            def _(part=part):
                ext_scr[part, 0:pad, :] = tail_scr[f, part]

            tail = ext_scr[part, tm:tm + pad, :]
            tail_scr[f, part] = tail
            tail_ref[0, part] = tail
            u1 = ext_scr[part, pad - 1:pad - 1 + tm, :]
            u2 = ext_scr[part, pad - 2:pad - 2 + tm, :]
        cw = cw_ref[...]
        conv.append(cb_ref[...] + cw[0:1, :] * u2 + cw[1:2, :] * u1 + cw[2:3, :] * u)
    z = jax.nn.silu(conv[0]) * conv[1]
    o_ref[...] += _dot(z.astype(BF16), wd_ref[...].astype(BF16))


def _conv_ffn(x, g, w_up, conv_w, conv_b, w_down, tm, seq=None, sample_state=None, t_new=None):
    m, d = x.shape
    d_ff = w_down.shape[0]
    tf = FFN_COL_TILE
    nf = d_ff // tf
    sample = sample_state is not None
    cb = conv_b.reshape(1, 2 * d_ff)
    in_specs = [
        pl.BlockSpec((tm, d), lambda i, f: (i, 0)),
        pl.BlockSpec((1, d), lambda i, f: (0, 0)),
        pl.BlockSpec((d, tf), lambda i, f: (0, f)),
        pl.BlockSpec((d, tf), lambda i, f: (0, nf + f)),
        pl.BlockSpec((CONV_W, tf), lambda i, f: (0, f)),
        pl.BlockSpec((CONV_W, tf), lambda i, f: (0, nf + f)),
        pl.BlockSpec((1, tf), lambda i, f: (0, f)),
        pl.BlockSpec((1, tf), lambda i, f: (0, nf + f)),
        pl.BlockSpec((tf, d), lambda i, f: (f, 0)),
    ]
    args = [x, g, w_up, w_up, conv_w, conv_w, cb, cb, w_down]
    scratch = [pltpu.VMEM((tm, d), BF16), pltpu.VMEM((2, tm + SUBLANES, tf), F32)]
    if sample:
        tiles_per_seq = None
        in_specs += [pl.BlockSpec((2, tm, tf), lambda i, f: (0, i, f))] * 2
        args += list(sample_state)
        out_specs = [pl.BlockSpec((tm, d), lambda i, f: (i, 0)),
                     pl.BlockSpec((2, tm, tf), lambda i, f: (0, i, f))]
        out_shape = [jax.ShapeDtypeStruct((m, d), F32), jax.ShapeDtypeStruct((2, m, d_ff), F32)]
    else:
        tiles_per_seq = seq // tm
        out_specs = [pl.BlockSpec((tm, d), lambda i, f: (i, 0)),
                     pl.BlockSpec((1, 2, SUBLANES, tf), lambda i, f: (i, 0, 0, f))]
        out_shape = [jax.ShapeDtypeStruct((m, d), F32),
                     jax.ShapeDtypeStruct((m // tm, 2, SUBLANES, d_ff), F32)]
        scratch.append(pltpu.VMEM((nf, 2, SUBLANES, tf), F32))
    kernel = functools.partial(_ffn_kernel, tm=tm, tiles_per_seq=tiles_per_seq, t_new=t_new)
    return pl.pallas_call(
        kernel,
        grid=(m // tm, nf),
        in_specs=in_specs,
        out_specs=out_specs,
        out_shape=out_shape,
        scratch_shapes=scratch,
        compiler_params=_cparams(("arbitrary", "arbitrary")),
        name="conv_ffn_sample" if sample else "conv_ffn",
    )(*args)


def _rope_tables(pos):
    half = HEAD_DIM // 2
    inv_freq = ROPE_THETA ** (-jnp.arange(half, dtype=F32) / half)
    ang = pos.astype(F32)[:, None] * inv_freq[None, :]
    cos, sin = jnp.cos(ang), jnp.sin(ang)
    return jnp.concatenate([cos, cos], axis=-1), jnp.concatenate([-sin, sin], axis=-1)


def kernel(x_prompt, x_sample, cache_kv_w128, cache_kv_w512, cache_kv_w2048, state_mlstm_C, state_mlstm_n, state_mlstm_m, state_ffn_conv, norm_mix, norm_ffn, attn_w_qkv, attn_q_norm, attn_k_norm, attn_w_o, mlstm_w_in, mlstm_b_gates, mlstm_norm_h, mlstm_w_out, ffn_w_up, ffn_conv_w, ffn_conv_b, ffn_w_down):
    batch, seq, d = x_prompt.shape
    dec_batch, t_new, _ = x_sample.shape
    depth = norm_mix.shape[0]
    caches = (cache_kv_w128, cache_kv_w512, cache_kv_w2048)
    dqk = state_mlstm_C.shape[-1]
    dv = state_mlstm_C.shape[-2]
    d_ff = ffn_w_down.shape[1]
    ms = dec_batch * t_new
    tm_p = ROW_TILE

    xp = x_prompt.reshape(batch * seq, d)
    xs = x_sample.reshape(ms, d)

    cos_p, sin_p = _rope_tables(jnp.arange(seq))
    cos_s, sin_s = _rope_tables(PAST_LEN + jnp.arange(t_new))
    cos_s, sin_s = jnp.tile(cos_s, (dec_batch, 1)), jnp.tile(sin_s, (dec_batch, 1))

    kv_p = [[] for _ in DIL_PATTERNS]
    kv_s = [[] for _ in DIL_PATTERNS]
    c_p, n_p, m_p, c_s, n_s, m_s = [], [], [], [], [], []
    conv_p, conv_s = [], []

    for layer in range(depth):
        g_mix = norm_mix[layer].reshape(1, d)
        if layer % 2 == 0:
            ia = layer // 2
            gains = jnp.stack([attn_q_norm[ia], attn_k_norm[ia]])
            qkv, *kvf = _qkv_proj(xp, g_mix, attn_w_qkv[ia], gains, cos_p, sin_p, tm_p)
            o0, l0 = _attn_prompt(qkv, 0, batch, seq, [])
            o1, l1 = _attn_prompt(qkv, 1, batch, seq, [])
            views = []
            for (o_g, l_g) in ((o0, l0), (o1, l1)):
                dil2 = DIL_PATTERNS[2][1]
                views.append((o_g.reshape(batch, seq // dil2, dil2 * ATT_W),
                              l_g.reshape(batch, seq // dil2, dil2 * LANES)))
            (comb,) = _attn_prompt(qkv, 2, batch, seq, views)
            xp = _matmul_residual(comb.reshape(batch * seq, ATT_W), attn_w_o[ia], xp, tm_p)
            for g, (win, _) in enumerate(DIL_PATTERNS):
                keep = min(win, seq)
                kv_p[g].append(kvf[g].reshape(batch, seq, 2, HEADS_A, HEAD_DIM)[:, seq - keep:])
            qkv_s, *kvf_s = _qkv_proj(xs, g_mix, attn_w_qkv[ia], gains, cos_s, sin_s, ms)
            comb_s = _attn_sample(qkv_s, [c[ia] for c in caches], dec_batch, t_new)
            xs = _matmul_residual(comb_s, attn_w_o[ia], xs, ms)
            for g in range(N_GROUPS):
                new = kvf_s[g].reshape(dec_batch, t_new, 2, HEADS_A, HEAD_DIM)
                kv_s[g].append(jnp.concatenate([caches[g][ia][:, t_new:], new], axis=1))
        else:
            ib = layer // 2
            w_in = mlstm_w_in[ib]
            n_main = 2 * HEADS_B * dqk + 2 * HEADS_B * dv
            wgt = w_in[:, n_main:].T
            bg = mlstm_b_gates[ib]
            n_gates = bg.shape[0]
            bg_row = jnp.zeros((1, LANES), F32).at[0, :n_gates].set(bg)
            bg_col = bg.reshape(n_gates, 1)
            gh = mlstm_norm_h[ib].reshape(1, HEADS_B * dv)
            qkv, og, gates, gates_t = _mlstm_in_proj(xp, g_mix, w_in, wgt, bg_row, bg_col, tm_p, dqk, dv)
            gated, c_f, n_f, m_f = _mlstm_core(qkv, og, gates, gates_t, gh, batch, seq,
                                               MLSTM_CHUNK, dqk, dv, None)
            xp = _matmul_residual(gated.reshape(batch * seq, HEADS_B * dv), mlstm_w_out[ib], xp, tm_p)
            c_p.append(c_f)
            n_p.append(n_f[:, :, 0, :])
            m_p.append(m_f[:, :, 0, 0])
            qkv, og, gates, gates_t = _mlstm_in_proj(xs, g_mix, w_in, wgt, bg_row, bg_col, ms, dqk, dv)
            tp = SAMPLE_PAD_T
            n_pad = tp - t_new

            def front_pad(a, value=0.0):
                a = a.reshape(dec_batch, t_new, a.shape[-1])
                a = jnp.pad(a, ((0, 0), (n_pad, 0), (0, 0)), constant_values=value)
                return a.reshape(dec_batch * tp, a.shape[-1])

            lane_is_ig = (jnp.arange(LANES) < HEADS_B)[None, None, :]
            g3 = gates.reshape(dec_batch, t_new, LANES)
            g3 = jnp.concatenate([jnp.where(lane_is_ig, NEG, 0.0).astype(F32)
                                  * jnp.ones((dec_batch, n_pad, 1), F32), g3], axis=1)
            gates_pad = g3.reshape(dec_batch * tp, LANES)
            gates_t_pad = jnp.swapaxes(g3[:, :, :n_gates], 1, 2)
            state = (state_mlstm_C[ib], state_mlstm_n[ib][:, :, None, :],
                     jnp.broadcast_to(state_mlstm_m[ib][:, :, None, None], (dec_batch, HEADS_B, 1, LANES)))
            gated, c_f, n_f, m_f = _mlstm_core(front_pad(qkv), front_pad(og), gates_pad, gates_t_pad, gh,
                                               dec_batch, tp, tp, dqk, dv, state)
            gated = gated[:, n_pad:].reshape(ms, HEADS_B * dv)
            xs = _matmul_residual(gated, mlstm_w_out[ib], xs, ms)
            c_s.append(c_f)
            n_s.append(n_f[:, :, 0, :])
            m_s.append(m_f[:, :, 0, 0])

        g_ffn = norm_ffn[layer].reshape(1, d)
        xp, tails = _conv_ffn(xp, g_ffn, ffn_w_up[layer], ffn_conv_w[layer], ffn_conv_b[layer],
                              ffn_w_down[layer], tm_p, seq=seq)
        tiles_per_seq = seq // tm_p
        tails = tails[tiles_per_seq - 1::tiles_per_seq, :, SUBLANES - (CONV_W - 1):, :]
        conv_p.append(jnp.swapaxes(tails, 1, 2).reshape(batch, CONV_W - 1, 2 * d_ff))
        st = state_ffn_conv[layer].reshape(dec_batch, CONV_W - 1, 2, d_ff)
        zeros = jnp.zeros((dec_batch, t_new, 2, d_ff), F32)

        def part_major(a):
            return jnp.transpose(a, (2, 0, 1, 3)).reshape(2, ms, d_ff)

        s1 = part_major(zeros.at[:, 0].set(st[:, 1]))
        s2 = part_major(zeros.at[:, 0].set(st[:, 0]).at[:, 1].set(st[:, 1]))
        xs, u_s = _conv_ffn(xs, g_ffn, ffn_w_up[layer], ffn_conv_w[layer], ffn_conv_b[layer],
                            ffn_w_down[layer], ms, sample_state=(s1, s2), t_new=t_new)
        u_s = jnp.moveaxis(u_s, 0, 1).reshape(dec_batch, t_new, 2 * d_ff)
        ext = jnp.concatenate([state_ffn_conv[layer], u_s], axis=1)
        conv_s.append(ext[:, t_new:])

    stack = jnp.stack
    return (xp.reshape(batch, seq, d), xs.reshape(dec_batch, t_new, d),
            stack(kv_p[0]), stack(kv_s[0]), stack(kv_p[1]), stack(kv_s[1]), stack(kv_p[2]), stack(kv_s[2]),
            stack(c_p), stack(c_s), stack(n_p), stack(n_s), stack(m_p), stack(m_s),
            stack(conv_p), stack(conv_s))
```

```python
import functools

import numpy as np
import jax
import jax.numpy as jnp
from jax import lax
from jax.experimental import pallas as pl
from jax.experimental.pallas import tpu as pltpu

F32 = jnp.float32
BF16 = jnp.bfloat16

DIL_PATTERNS = ((128, 1), (512, 4), (2048, 16))
N_GROUPS = len(DIL_PATTERNS)
HEADS_A = 8
HEAD_DIM = 128
ATT_W = HEADS_A * HEAD_DIM
BAND = 128
ROPE_THETA = 10000.0
PAST_LEN = 16384
HEADS_B = 8
GATE_CAP = 15.0
CONV_W = 3
EPS = 1e-6
NEG = -1e30

LANES = 128
SUBLANES = 8
VMEM_LIMIT_BYTES = 60 * 1024 * 1024
ROW_TILE = 1024
ATTN_ROW_TILE = 512
ROW_SUBTILE = 256
PERM_TILE = 256
PROJ_COL_TILE = 1024
FFN_COL_TILE = 512
MLSTM_CHUNK = 256
SAMPLE_PAD_T = 16


def _cparams(sem):
    return pltpu.CompilerParams(dimension_semantics=sem, vmem_limit_bytes=VMEM_LIMIT_BYTES)


def _rms_rows(x, g):
    ms = jnp.mean(x * x, axis=-1, keepdims=True)
    return (x * lax.rsqrt(ms + EPS)) * g


def _log_sigmoid(x):
    return jnp.minimum(x, 0.0) - jnp.log1p(jnp.exp(-jnp.abs(x)))


def _dot(a, b):
    return jnp.dot(a, b, preferred_element_type=F32)


def _dot_nt(a, b):
    return lax.dot_general(a, b, (((1,), (1,)), ((), ())), preferred_element_type=F32)


def _dot_tn(a, b):
    return lax.dot_general(a, b, (((0,), (0,)), ((), ())), preferred_element_type=F32)


def _sum(xs):
    return functools.reduce(lambda a, b: a + b, xs)


def _split3(x):
    hi = x.astype(BF16)
    r = x - hi.astype(F32)
    mid = r.astype(BF16)
    lo = (r - mid.astype(F32)).astype(BF16)
    return hi, mid, lo


def _select_rows(sel, x):
    return _sum([_dot(sel, p) for p in _split3(x)])


def _layer_spec(block, index_map, layer):
    return pl.BlockSpec((None,) + block, lambda *idx: (layer,) + index_map(*idx))


def _norm_kernel(x_ref, g_ref, h_ref):
    h_ref[...] = _rms_rows(x_ref[...], g_ref[...]).astype(BF16)


def _norm(x, gains, layer, tm):
    m, d = x.shape
    return pl.pallas_call(
        _norm_kernel,
        grid=(m // tm,),
        in_specs=[pl.BlockSpec((tm, d), lambda i: (i, 0)),
                  _layer_spec((1, d), lambda i: (0, 0), layer)],
        out_specs=pl.BlockSpec((tm, d), lambda i: (i, 0)),
        out_shape=jax.ShapeDtypeStruct((m, d), BF16),
        compiler_params=_cparams(("arbitrary",)),
        name="rms_norm",
    )(x, gains)


def _group_proj_kernel(*refs, rs, has_perm):
    h_ref, wq_ref, wk_ref, wv_ref, gains_ref, cos_ref, sin_ref = refs[:7]
    rest = refs[7:]
    if has_perm:
        perm_ref, rest = rest[0], rest[1:]
    q_ref, k_ref, v_ref, kv_ref = rest
    tm = h_ref.shape[0]
    kv_rows = 2 * HEADS_A

    def rope_heads(y, gain, cos, sin):
        outs = []
        for hh in range(HEADS_A):
            yn = _rms_rows(y[:, hh * HEAD_DIM:(hh + 1) * HEAD_DIM], gain)
            outs.append(yn * cos + pltpu.roll(yn, HEAD_DIM // 2, 1) * sin)
        return outs

    for r in range(tm // rs):
        rows = slice(r * rs, (r + 1) * rs)
        h = h_ref[rows, :]
        cos = cos_ref[rows, :]
        sin = sin_ref[rows, :]
        q = rope_heads(_dot(h, wq_ref[...]), gains_ref[0:1, :], cos, sin)
        k = rope_heads(_dot(h, wk_ref[...]), gains_ref[1:2, :], cos, sin)
        vfull = _dot(h, wv_ref[...])
        v = [vfull[:, hh * HEAD_DIM:(hh + 1) * HEAD_DIM] for hh in range(HEADS_A)]
        for kind, heads in enumerate((k, v)):
            for hh in range(HEADS_A):
                kv_ref[pl.ds(r * rs * kv_rows + kind * HEADS_A + hh, rs, stride=kv_rows), :] = heads[hh]
        for heads, ref in ((q, q_ref), (k, k_ref), (v, v_ref)):
            val = jnp.concatenate(heads, axis=1).astype(BF16)
            if has_perm:
                val = _dot(perm_ref[...], val).astype(BF16)
            ref[rows, :] = val


def _group_proj(h, w_all, layer, g, gains, cos, sin, perm, tm, rs):
    m, d = h.shape
    n_pos_tiles = cos.shape[0] // tm
    w_specs = [_layer_spec((d, ATT_W), (lambda kind: lambda i: (0, 3 * g + kind))(kind), layer)
               for kind in range(3)]
    in_specs = [pl.BlockSpec((tm, d), lambda i: (i, 0))] + w_specs + [
        pl.BlockSpec((2, HEAD_DIM), lambda i: (0, 0)),
        pl.BlockSpec((tm, HEAD_DIM), lambda i: (i % n_pos_tiles, 0)),
        pl.BlockSpec((tm, HEAD_DIM), lambda i: (i % n_pos_tiles, 0)),
    ]
    args = [h, w_all, w_all, w_all, gains, cos, sin]
    if perm is not None:
        in_specs.append(pl.BlockSpec(perm.shape, lambda i: (0, 0)))
        args.append(perm)
    row_spec = pl.BlockSpec((tm, ATT_W), lambda i: (i, 0))
    kv_rows = 2 * HEADS_A
    kernel = functools.partial(_group_proj_kernel, rs=rs, has_perm=perm is not None)
    return pl.pallas_call(
        kernel,
        grid=(m // tm,),
        in_specs=in_specs,
        out_specs=[row_spec, row_spec, row_spec, pl.BlockSpec((tm * kv_rows, HEAD_DIM), lambda i: (i, 0))],
        out_shape=[jax.ShapeDtypeStruct((m, ATT_W), BF16)] * 3
        + [jax.ShapeDtypeStruct((m * kv_rows, HEAD_DIM), F32)],
        compiler_params=_cparams(("arbitrary",)),
        name=f"attn_proj_g{g}",
    )(*args)


def _attn_prompt_kernel(*refs, tq, n_chunks, has_prev):
    q_ref, k_ref, v_ref = refs[:3]
    rest = refs[3:]
    if has_prev:
        kp_ref, vp_ref = rest[:2]
        rest = rest[2:]
    o_ref, lse_ref, qbuf, kbuf, vbuf, obuf, lbuf = rest
    n = pl.program_id(2)
    scale = HEAD_DIM ** -0.5
    cr = tq // n_chunks

    if has_prev:
        kbuf[0:BAND, :] = kp_ref[0, 0, 0]
        vbuf[0:BAND, :] = vp_ref[0, 0, 0]
    else:
        kbuf[0:BAND, :] = jnp.zeros((BAND, ATT_W), BF16)
        vbuf[0:BAND, :] = jnp.zeros((BAND, ATT_W), BF16)
    for t in range(n_chunks):
        qbuf[t * cr:(t + 1) * cr, :] = q_ref[0, t, 0]
        kbuf[BAND + t * cr:BAND + (t + 1) * cr, :] = k_ref[0, t, 0]
        vbuf[BAND + t * cr:BAND + (t + 1) * cr, :] = v_ref[0, t, 0]

    qi = lax.broadcasted_iota(jnp.int32, (BAND, 2 * BAND), 0)
    kj = lax.broadcasted_iota(jnp.int32, (BAND, 2 * BAND), 1)
    bias = jnp.where((kj >= qi) & (kj <= qi + BAND), 0.0, NEG).astype(F32)
    first_pen = jnp.where(n == 0, NEG, 0.0).astype(F32) if has_prev else NEG
    bias_first = bias + jnp.where(kj < BAND, first_pen, 0.0)
    lane = lax.broadcasted_iota(jnp.int32, (BAND, LANES), 1)

    for c in range(tq // BAND):
        rows = slice(c * BAND, (c + 1) * BAND)
        krows = slice(c * BAND, c * BAND + 2 * BAND)
        b = bias_first if c == 0 else bias
        lse_tile = jnp.zeros((BAND, LANES), F32)
        for h in range(HEADS_A):
            cols = slice(h * HEAD_DIM, (h + 1) * HEAD_DIM)
            s = _dot_nt(qbuf[rows, cols], kbuf[krows, cols]) * scale + b
            mx = jnp.max(s, axis=-1, keepdims=True)
            p = jnp.exp(s - mx)
            den = jnp.sum(p, axis=-1, keepdims=True)
            obuf[rows, cols] = _dot(p.astype(BF16), vbuf[krows, cols]) / den
            lse_tile = jnp.where(lane == h, mx + jnp.log(den), lse_tile)
        lbuf[rows, :] = lse_tile

    for t in range(n_chunks):
        o_ref[0, t, 0] = obuf[t * cr:(t + 1) * cr, :]
        lse_ref[0, t, 0] = lbuf[t * cr:(t + 1) * cr, :]


def _attn_prompt(q, k, v, g, batch, seq):
    _, dil = DIL_PATTERNS[g]
    l = seq // dil
    if dil == 1:
        tq, n_chunks = min(l, 512), 1
        view = (batch, l // tq, 1, tq)
        block = (1, 1, 1, tq)
        index = lambda b, r, n: (b, n, 0, 0, 0)
    else:
        tq, n_chunks = l, seq // PERM_TILE
        view = (batch, n_chunks, dil, PERM_TILE // dil)
        block = (1, n_chunks, 1, PERM_TILE // dil)
        index = lambda b, r, n: (b, 0, r, 0, 0)
    nq = l // tq
    has_prev = nq > 1

    def spec(width):
        return pl.BlockSpec(block + (width,), index)

    in_specs = [spec(ATT_W)] * 3
    args = [a.reshape(view + (ATT_W,)) for a in (q, k, v)]
    if has_prev:
        sub = tq // BAND
        prev_spec = pl.BlockSpec((1, 1, 1, BAND, ATT_W),
                                 lambda b, r, n: (b, jnp.maximum(n * sub - 1, 0), 0, 0, 0))
        in_specs += [prev_spec] * 2
        args += [a.reshape(batch, l // BAND, 1, BAND, ATT_W) for a in (k, v)]
    kernel = functools.partial(_attn_prompt_kernel, tq=tq, n_chunks=n_chunks, has_prev=has_prev)
    o, lse = pl.pallas_call(
        kernel,
        grid=(batch, dil, nq),
        in_specs=in_specs,
        out_specs=[spec(ATT_W), spec(LANES)],
        out_shape=[jax.ShapeDtypeStruct(view + (ATT_W,), F32), jax.ShapeDtypeStruct(view + (LANES,), F32)],
        scratch_shapes=[pltpu.VMEM((tq, ATT_W), BF16), pltpu.VMEM((BAND + tq, ATT_W), BF16),
                        pltpu.VMEM((BAND + tq, ATT_W), BF16), pltpu.VMEM((tq, ATT_W), F32),
                        pltpu.VMEM((tq, LANES), F32)],
        compiler_params=_cparams(("arbitrary", "arbitrary", "arbitrary")),
        name=f"attn_prompt_g{g}",
    )(*args)
    return o.reshape(batch * seq, ATT_W), lse.reshape(batch * seq, LANES)


def _combine_proj_kernel(o0_ref, l0_ref, o1_ref, l1_ref, o2_ref, l2_ref, s1_ref, s2_ref,
                         w_ref, x_ref, out_ref, *, rs):
    tm = x_ref.shape[0]
    for r in range(tm // rs):
        rows = slice(r * rs, (r + 1) * rs)
        os_ = [o0_ref[rows, :], _select_rows(s1_ref[...], o1_ref[rows, :]),
               _select_rows(s2_ref[...], o2_ref[rows, :])]
        ls = [l0_ref[rows, :], _select_rows(s1_ref[...], l1_ref[rows, :]),
              _select_rows(s2_ref[...], l2_ref[rows, :])]
        top = functools.reduce(jnp.maximum, ls)
        es = [jnp.exp(l - top) for l in ls]
        tot = _sum(es)
        alphas = [e / tot for e in es]
        heads = []
        for h in range(HEADS_A):
            cols = slice(h * HEAD_DIM, (h + 1) * HEAD_DIM)
            heads.append(_sum([a[:, h:h + 1] * o[:, cols] for a, o in zip(alphas, os_)]))
        comb = jnp.concatenate(heads, axis=1).astype(BF16)
        out_ref[rows, :] = x_ref[rows, :] + _dot(comb, w_ref[...])


def _combine_proj(os_, ls, sels, w_all, layer, x, tm, rs):
    m, d = x.shape
    o_spec = pl.BlockSpec((tm, ATT_W), lambda i: (i, 0))
    l_spec = pl.BlockSpec((tm, LANES), lambda i: (i, 0))
    sel_spec = pl.BlockSpec((rs, rs), lambda i: (0, 0))
    x_spec = pl.BlockSpec((tm, d), lambda i: (i, 0))
    kernel = functools.partial(_combine_proj_kernel, rs=rs)
    return pl.pallas_call(
        kernel,
        grid=(m // tm,),
        in_specs=[o_spec, l_spec, o_spec, l_spec, o_spec, l_spec, sel_spec, sel_spec,
                  _layer_spec((ATT_W, d), lambda i: (0, 0), layer), x_spec],
        out_specs=x_spec,
        out_shape=jax.ShapeDtypeStruct((m, d), F32),
        compiler_params=_cparams(("arbitrary",)),
        name="attn_combine_proj",
    )(os_[0], ls[0], os_[1], ls[1], os_[2], ls[2], sels[0], sels[1], w_all, x)


def _attn_sample_kernel(q_ref, kvn0_ref, kvn1_ref, kvn2_ref, c0_ref, c1_ref, c2_ref, o_ref, *, t_new):
    scale = HEAD_DIM ** -0.5
    caches = (c0_ref, c1_ref, c2_ref)
    news = (kvn0_ref, kvn1_ref, kvn2_ref)
    key_u = lax.broadcasted_iota(jnp.int32, (BAND, HEADS_A, 1), 0)

    for t in range(t_new):
        o_g, lse_g = [], []
        for g, (_, dil) in enumerate(DIL_PATTERNS):
            q = q_ref[g, 0, t]
            res = 0 if dil == 1 else t
            s_c = jnp.sum(caches[g][:, res, 0] * q[None], axis=-1, keepdims=True) * scale
            if dil == 1:
                s_c = jnp.where(key_u >= t, s_c, NEG)
                new_ts = list(range(t + 1))
            else:
                new_ts = [t]
            s_n = [jnp.sum(q * news[g][0, t2, 0], axis=-1, keepdims=True) * scale for t2 in new_ts]
            mx = functools.reduce(jnp.maximum, s_n, jnp.max(s_c, axis=0))
            p_c = jnp.exp(s_c - mx[None])
            den = jnp.sum(p_c, axis=0)
            acc = jnp.sum(p_c * caches[g][:, res, 1], axis=0)
            for t2, s in zip(new_ts, s_n):
                p_n = jnp.exp(s - mx)
                den = den + p_n
                acc = acc + p_n * news[g][0, t2, 1]
            o_g.append(acc / den)
            lse_g.append(mx + jnp.log(den))
        top = functools.reduce(jnp.maximum, lse_g)
        es = [jnp.exp(l - top) for l in lse_g]
        tot = _sum(es)
        o_ref[0, t] = _sum([(e / tot) * o for e, o in zip(es, o_g)])


def _attn_sample(q, kv_new, caches, layer, batch, t_new):
    tile = (HEADS_A, HEAD_DIM)
    in_specs = [pl.BlockSpec((N_GROUPS, 1, t_new) + tile, lambda b: (0, b, 0, 0, 0))]
    in_specs += [pl.BlockSpec((1, t_new, 2) + tile, lambda b: (b, 0, 0, 0, 0))] * N_GROUPS
    args = [q] + list(kv_new)
    for (win, dil), c in zip(DIL_PATTERNS, caches):
        n_res = min(dil, t_new)
        in_specs.append(pl.BlockSpec((None, None, BAND, n_res, 2) + tile,
                                     lambda b: (layer, b, 0, 0, 0, 0, 0)))
        args.append(c.reshape(c.shape[:2] + (BAND, dil, 2) + tile))
    kernel = functools.partial(_attn_sample_kernel, t_new=t_new)
    return pl.pallas_call(
        kernel,
        grid=(batch,),
        in_specs=in_specs,
        out_specs=pl.BlockSpec((1, t_new) + tile, lambda b: (b, 0, 0, 0)),
        out_shape=jax.ShapeDtypeStruct((batch, t_new) + tile, F32),
        compiler_params=_cparams(("arbitrary",)),
        name="attn_sample",
    )(*args)


def _matmul_residual_kernel(a_ref, w_ref, x_ref, o_ref):
    o_ref[...] = x_ref[...] + _dot(a_ref[...].astype(BF16), w_ref[...])


def _matmul_residual(a, w_all, layer, x, tm):
    m, k = a.shape
    n = w_all.shape[-1]
    tn = PROJ_COL_TILE
    return pl.pallas_call(
        _matmul_residual_kernel,
        grid=(m // tm, n // tn),
        in_specs=[
            pl.BlockSpec((tm, k), lambda i, j: (i, 0)),
            _layer_spec((k, tn), lambda i, j: (0, j), layer),
            pl.BlockSpec((tm, tn), lambda i, j: (i, j)),
        ],
        out_specs=pl.BlockSpec((tm, tn), lambda i, j: (i, j)),
        out_shape=jax.ShapeDtypeStruct((m, n), F32),
        compiler_params=_cparams(("arbitrary", "arbitrary")),
        name="matmul_residual",
    )(a, w_all, x)


def _gate_act(pre, is_input_gate):
    capped = GATE_CAP * jnp.tanh(pre / GATE_CAP)
    return jnp.where(is_input_gate, capped, _log_sigmoid(capped))


def _mlstm_in_kernel(x_ref, g_ref, w_ref, wg_ref, wgt_ref, bg_ref, bgt_ref,
                     qkv_ref, og_ref, gates_ref, gatest_ref, h_scr, *, k_tiles, n_qkv_tiles, k_scale):
    j = pl.program_id(1)

    @pl.when(j == 0)
    def _():
        h = _rms_rows(x_ref[...], g_ref[...]).astype(BF16)
        h_scr[...] = h
        n_gates = wgt_ref.shape[0]
        pre = _dot(h, wg_ref[...]) + bg_ref[...]
        lane = lax.broadcasted_iota(jnp.int32, pre.shape, 1)
        gates_ref[...] = jnp.where(lane < n_gates, _gate_act(pre, lane < n_gates // 2), 0.0)
        pre_t = _dot_nt(wgt_ref[...], h) + bgt_ref[...]
        row = lax.broadcasted_iota(jnp.int32, pre_t.shape, 0)
        gatest_ref[...] = _gate_act(pre_t, row < n_gates // 2)

    y = _dot(h_scr[...], w_ref[...])

    @pl.when(j < n_qkv_tiles)
    def _():
        is_k = (j >= k_tiles[0]) & (j < k_tiles[1])
        qkv_ref[...] = (y * jnp.where(is_k, k_scale, 1.0).astype(F32)).astype(BF16)

    @pl.when(j >= n_qkv_tiles)
    def _():
        og_ref[...] = y


def _mlstm_in_proj(x, gains, layer, w_all, ib, wgt_all, bg_all, bgt_all, tm, dqk, dv):
    m, d = x.shape
    tn = PROJ_COL_TILE
    qk_w, v_w = HEADS_B * dqk, HEADS_B * dv
    n_main = 2 * qk_w + 2 * v_w
    n_gates = w_all.shape[-1] - n_main
    n_qkv_tiles = (2 * qk_w + v_w) // tn
    nj = n_main // tn
    kernel = functools.partial(_mlstm_in_kernel, k_tiles=(qk_w // tn, 2 * qk_w // tn),
                               n_qkv_tiles=n_qkv_tiles, k_scale=dqk ** -0.5)
    return pl.pallas_call(
        kernel,
        grid=(m // tm, nj),
        in_specs=[
            pl.BlockSpec((tm, d), lambda i, j: (i, 0)),
            _layer_spec((1, d), lambda i, j: (0, 0), layer),
            _layer_spec((d, tn), lambda i, j: (0, j), ib),
            _layer_spec((d, LANES), lambda i, j: (0, n_main // LANES), ib),
            _layer_spec((n_gates, d), lambda i, j: (0, 0), ib),
            _layer_spec((1, LANES), lambda i, j: (0, 0), ib),
            _layer_spec((n_gates, 1), lambda i, j: (0, 0), ib),
        ],
        out_specs=[
            pl.BlockSpec((tm, tn), lambda i, j: (i, jnp.minimum(j, n_qkv_tiles - 1))),
            pl.BlockSpec((tm, tn), lambda i, j: (i, jnp.maximum(j - n_qkv_tiles, 0))),
            pl.BlockSpec((tm, LANES), lambda i, j: (i, 0)),
            pl.BlockSpec((n_gates, tm), lambda i, j: (0, i)),
        ],
        out_shape=[
            jax.ShapeDtypeStruct((m, 2 * qk_w + v_w), BF16),
            jax.ShapeDtypeStruct((m, v_w), F32),
            jax.ShapeDtypeStruct((m, LANES), F32),
            jax.ShapeDtypeStruct((n_gates, m), F32),
        ],
        scratch_shapes=[pltpu.VMEM((tm, d), BF16)],
        compiler_params=_cparams(("arbitrary", "arbitrary")),
        name="mlstm_in_proj",
    )(x, gains, w_all, w_all, wgt_all, bg_all, bgt_all)


def _mlstm_core_kernel(*refs, chunk, dqk, dv, has_state):
    q_ref, k_ref, v_ref, og_ref, gates_ref, gatest_ref, gh_ref = refs[:7]
    rest = refs[7:]
    if has_state:
        c0_ref, n0_ref, m0_ref = rest[:3]
        rest = rest[3:]
    out_ref, c_out, n_out, m_out, c_scr, n_scr, m_scr = rest
    c = pl.program_id(1)
    n_heads = c_scr.shape[0]

    @pl.when(c == 0)
    def _():
        if has_state:
            c_scr[...] = c0_ref[0]
            n_scr[...] = n0_ref[0]
            m_scr[...] = m0_ref[0]
        else:
            c_scr[...] = jnp.zeros_like(c_scr)
            n_scr[...] = jnp.zeros_like(n_scr)
            m_scr[...] = jnp.zeros_like(m_scr)

    row = lax.broadcasted_iota(jnp.int32, (chunk, chunk), 0)
    col = lax.broadcasted_iota(jnp.int32, (chunk, chunk), 1)
    causal = col <= row
    tri_lo = jnp.where(causal, 1.0, 0.0).astype(BF16)
    tri_up = jnp.where(row <= col, 1.0, 0.0).astype(BF16)

    gates = gates_ref[0]
    gates_t = gatest_ref[0] if len(gatest_ref.shape) == 3 else gatest_ref[...]
    b_cols = _sum([_dot(tri_lo, p) for p in _split3(gates)])
    b_rows = _sum([_dot(p, tri_up) for p in _split3(gates_t)])

    for hh in range(n_heads):
        q = q_ref[0, :, hh * dqk:(hh + 1) * dqk]
        k = k_ref[0, :, hh * dqk:(hh + 1) * dqk]
        v = v_ref[0, :, hh * dv:(hh + 1) * dv]
        ig_col = gates[:, hh:hh + 1]
        b_col = b_cols[:, n_heads + hh:n_heads + hh + 1]
        ig_row = gates_t[hh:hh + 1, :]
        b_row = b_rows[n_heads + hh:n_heads + hh + 1, :]
        m_prev = m_scr[hh][:, 0:1]
        c_state = c_scr[hh]
        n_state = n_scr[hh]

        dm = jnp.where(causal, b_col + (ig_row - b_row), NEG)
        inter = b_col + m_prev
        m_t = jnp.maximum(inter, jnp.max(dm, axis=-1, keepdims=True))
        w_inter = jnp.exp(inter - m_t)
        p = jnp.exp(dm - m_t) * _dot_nt(q, k)
        num = w_inter * _dot_nt(q, c_state.astype(BF16)) + _dot(p.astype(BF16), v)
        nq = jnp.sum(q.astype(F32) * n_state, axis=-1, keepdims=True)
        den = w_inter * nq + jnp.sum(p, axis=-1, keepdims=True)
        hid = num / jnp.maximum(jnp.abs(den), jnp.exp(-m_t))
        hn = hid * lax.rsqrt(jnp.mean(hid * hid, axis=-1, keepdims=True) + EPS)
        cols = slice(hh * dv, (hh + 1) * dv)
        gated = hn * gh_ref[:, cols] * jax.nn.sigmoid(og_ref[0, :, cols])
        out_ref[0, :, cols] = gated.astype(BF16)

        b_last = b_col[chunk - 1:chunk, :]
        dec = b_last - b_col + ig_col
        m_new = jnp.maximum(b_last + m_prev, jnp.max(dec, axis=0, keepdims=True))
        w_c = jnp.exp(b_last + m_prev - m_new)
        w_s = jnp.exp(dec - m_new)
        vw = (v.astype(F32) * w_s).astype(BF16)
        c_scr[hh] = w_c * c_state + _dot_tn(vw, k)
        n_scr[hh] = w_c * n_state + jnp.sum(k.astype(F32) * w_s, axis=0, keepdims=True)
        m_scr[hh] = jnp.broadcast_to(m_new, (1, LANES))

    @pl.when(c == pl.num_programs(1) - 1)
    def _():
        c_out[0] = c_scr[...]
        n_out[0] = n_scr[...]
        m_out[0] = m_scr[...]


def _mlstm_core(qkv, og, gates, gates_t, gh_all, ib, batch, seq, chunk, dqk, dv, state):
    n_heads = HEADS_B
    qk_w, v_w = n_heads * dqk, n_heads * dv
    nc = seq // chunk
    qkv3 = qkv.reshape(batch, seq, 2 * qk_w + v_w)
    state_specs = [
        pl.BlockSpec((1, n_heads, dv, dqk), lambda b, c: (b, 0, 0, 0)),
        pl.BlockSpec((1, n_heads, 1, dqk), lambda b, c: (b, 0, 0, 0)),
        pl.BlockSpec((1, n_heads, 1, LANES), lambda b, c: (b, 0, 0, 0)),
    ]
    in_specs = [
        pl.BlockSpec((1, chunk, qk_w), lambda b, c: (b, c, 0)),
        pl.BlockSpec((1, chunk, qk_w), lambda b, c: (b, c, 1)),
        pl.BlockSpec((1, chunk, v_w), lambda b, c: (b, c, 2 * qk_w // v_w)),
        pl.BlockSpec((1, chunk, v_w), lambda b, c: (b, c, 0)),
        pl.BlockSpec((1, chunk, LANES), lambda b, c: (b, c, 0)),
        pl.BlockSpec((2 * n_heads, chunk), lambda b, c: (0, b * nc + c)) if gates_t.ndim == 2
        else pl.BlockSpec((1, 2 * n_heads, chunk), lambda b, c: (b * nc + c, 0, 0)),
        _layer_spec((1, v_w), lambda b, c: (0, 0), ib),
    ]
    args = [qkv3, qkv3, qkv3, og.reshape(batch, seq, v_w), gates.reshape(batch, seq, LANES), gates_t, gh_all]
    if state is not None:
        in_specs += state_specs
        args += list(state)
    kernel = functools.partial(_mlstm_core_kernel, chunk=chunk, dqk=dqk, dv=dv,
                               has_state=state is not None)
    return pl.pallas_call(
        kernel,
        grid=(batch, nc),
        in_specs=in_specs,
        out_specs=[pl.BlockSpec((1, chunk, v_w), lambda b, c: (b, c, 0))] + state_specs,
        out_shape=[
            jax.ShapeDtypeStruct((batch, seq, v_w), BF16),
            jax.ShapeDtypeStruct((batch, n_heads, dv, dqk), F32),
            jax.ShapeDtypeStruct((batch, n_heads, 1, dqk), F32),
            jax.ShapeDtypeStruct((batch, n_heads, 1, LANES), F32),
        ],
        scratch_shapes=[pltpu.VMEM((n_heads, dv, dqk), F32), pltpu.VMEM((n_heads, 1, dqk), F32),
                        pltpu.VMEM((n_heads, 1, LANES), F32)],
        compiler_params=_cparams(("arbitrary", "arbitrary")),
        name="mlstm_core",
    )(*args)


def _ffn_kernel(*refs, rs, tiles_per_seq, t_new):
    x_ref, g_ref, wa_ref, wb_ref, cwa_ref, cwb_ref, cba_ref, cbb_ref, wd_ref = refs[:9]
    rest = refs[9:]
    sample = t_new is not None
    if sample:
        s1_ref, s2_ref, o_ref, u_ref, h_scr = rest
    else:
        o_ref, tail_ref, h_scr, tail_scr = rest
    i = pl.program_id(0)
    f = pl.program_id(1)
    tm = x_ref.shape[0]
    n_sub = tm // rs

    @pl.when(f == 0)
    def _():
        x = x_ref[...]
        h_scr[...] = _rms_rows(x, g_ref[...]).astype(BF16)
        o_ref[...] = x

    parts = ((wa_ref, cwa_ref, cba_ref), (wb_ref, cwb_ref, cbb_ref))
    if not sample:
        @pl.when((i == 0) & (f == 0))
        def _():
            tail_scr[...] = jnp.zeros_like(tail_scr)

        seq_start = i % tiles_per_seq == 0
        prev = []
        for part in range(2):
            t = jnp.where(seq_start, 0.0, tail_scr[f, part])
            prev.append((t[SUBLANES - 2:SUBLANES - 1, :], t[SUBLANES - 1:SUBLANES, :]))

    for r in range(n_sub):
        rows = slice(r * rs, (r + 1) * rs)
        h = h_scr[rows, :]
        conv = []
        for part, (w_ref, cw_ref, cb_ref) in enumerate(parts):
            u = _dot(h, w_ref[...])
            ridx = lax.broadcasted_iota(jnp.int32, u.shape, 0)
            r1 = pltpu.roll(u, 1, 0)
            r2 = pltpu.roll(u, 2, 0)
            if sample:
                u_ref[part] = u
                pos = ridx % t_new
                u1 = jnp.where(pos >= 1, r1, s1_ref[part])
                u2 = jnp.where(pos >= 2, r2, s2_ref[part])
            else:
                p2, p1 = prev[part]
                u1 = jnp.where(ridx == 0, p1, r1)
                u2 = jnp.where(ridx == 0, p2, jnp.where(ridx == 1, p1, r2))
                prev[part] = (u[rs - 2:rs - 1, :], u[rs - 1:rs, :])
                if r == n_sub - 1:
                    tail = u[rs - SUBLANES:rs, :]
                    tail_scr[f, part] = tail
                    tail_ref[0, part] = tail
            cw = cw_ref[...]
            conv.append(cb_ref[...] + cw[0:1, :] * u2 + cw[1:2, :] * u1 + cw[2:3, :] * u)
        z = jax.nn.silu(conv[0]) * conv[1]
        o_ref[rows, :] += _dot(z.astype(BF16), wd_ref[...])


def _conv_ffn(x, gains, layer, w_up_all, conv_w_all, conv_b_all, w_down_all, tm, rs,
              seq=None, sample_state=None, t_new=None):
    m, d = x.shape
    d_ff = w_down_all.shape[1]
    tf = FFN_COL_TILE
    nf = d_ff // tf
    sample = sample_state is not None
    in_specs = [
        pl.BlockSpec((tm, d), lambda i, f: (i, 0)),
        _layer_spec((1, d), lambda i, f: (0, 0), layer),
        _layer_spec((d, tf), lambda i, f: (0, f), layer),
        _layer_spec((d, tf), lambda i, f: (0, nf + f), layer),
        _layer_spec((CONV_W, tf), lambda i, f: (0, f), layer),
        _layer_spec((CONV_W, tf), lambda i, f: (0, nf + f), layer),
        _layer_spec((1, tf), lambda i, f: (0, f), layer),
        _layer_spec((1, tf), lambda i, f: (0, nf + f), layer),
        _layer_spec((tf, d), lambda i, f: (f, 0), layer),
    ]
    args = [x, gains, w_up_all, w_up_all, conv_w_all, conv_w_all, conv_b_all, conv_b_all, w_down_all]
    scratch = [pltpu.VMEM((tm, d), BF16)]
    if sample:
        tiles_per_seq = None
        in_specs += [pl.BlockSpec((2, tm, tf), lambda i, f: (0, i, f))] * 2
        args += list(sample_state)
        out_specs = [pl.BlockSpec((tm, d), lambda i, f: (i, 0)),
                     pl.BlockSpec((2, tm, tf), lambda i, f: (0, i, f))]
        out_shape = [jax.ShapeDtypeStruct((m, d), F32), jax.ShapeDtypeStruct((2, m, d_ff), F32)]
    else:
        tiles_per_seq = seq // tm
        out_specs = [pl.BlockSpec((tm, d), lambda i, f: (i, 0)),
                     pl.BlockSpec((1, 2, SUBLANES, tf), lambda i, f: (i, 0, 0, f))]
        out_shape = [jax.ShapeDtypeStruct((m, d), F32),
                     jax.ShapeDtypeStruct((m // tm, 2, SUBLANES, d_ff), F32)]
        scratch.append(pltpu.VMEM((nf, 2, SUBLANES, tf), F32))
    kernel = functools.partial(_ffn_kernel, rs=rs, tiles_per_seq=tiles_per_seq, t_new=t_new)
    return pl.pallas_call(
        kernel,
        grid=(m // tm, nf),
        in_specs=in_specs,
        out_specs=out_specs,
        out_shape=out_shape,
        scratch_shapes=scratch,
        compiler_params=_cparams(("arbitrary", "arbitrary")),
        name="conv_ffn_sample" if sample else "conv_ffn",
    )(*args)


def _rope_tables(pos):
    half = HEAD_DIM // 2
    inv_freq = ROPE_THETA ** (-jnp.arange(half, dtype=F32) / half)
    ang = pos.astype(F32)[:, None] * inv_freq[None, :]
    cos, sin = jnp.cos(ang), jnp.sin(ang)
    return jnp.concatenate([cos, cos], axis=-1), jnp.concatenate([-sin, sin], axis=-1)


def _perm_matrices(dil):
    pos = np.arange(PERM_TILE)
    slot = (pos % dil) * (PERM_TILE // dil) + pos // dil
    to_residue_major = np.zeros((PERM_TILE, PERM_TILE), np.float32)
    to_residue_major[slot, pos] = 1.0
    return jnp.asarray(to_residue_major, BF16), jnp.asarray(to_residue_major.T, BF16)


def kernel(x_prompt, x_sample, cache_kv_w128, cache_kv_w512, cache_kv_w2048, state_mlstm_C, state_mlstm_n, state_mlstm_m, state_ffn_conv, norm_mix, norm_ffn, attn_w_qkv, attn_q_norm, attn_k_norm, attn_w_o, mlstm_w_in, mlstm_b_gates, mlstm_norm_h, mlstm_w_out, ffn_w_up, ffn_conv_w, ffn_conv_b, ffn_w_down):
    batch, seq, d = x_prompt.shape
    dec_batch, t_new, _ = x_sample.shape
    depth = norm_mix.shape[0]
    caches = (cache_kv_w128, cache_kv_w512, cache_kv_w2048)
    dqk = state_mlstm_C.shape[-1]
    dv = state_mlstm_C.shape[-2]
    d_ff = ffn_w_down.shape[1]
    ms = dec_batch * t_new
    tm_p = ROW_TILE
    tile = (HEADS_A, HEAD_DIM)

    xp = x_prompt.reshape(batch * seq, d)
    xs = x_sample.reshape(ms, d)

    w_qkv, w_o, w_in, w_out, w_up, w_down = (a.astype(BF16) for a in (
        attn_w_qkv, attn_w_o, mlstm_w_in, mlstm_w_out, ffn_w_up, ffn_w_down))
    g_mix = norm_mix[:, None, :]
    g_ffn = norm_ffn[:, None, :]
    conv_b = ffn_conv_b[:, None, :]
    n_main = 2 * HEADS_B * dqk + 2 * HEADS_B * dv
    n_gates = mlstm_b_gates.shape[-1]
    wgt = jnp.swapaxes(w_in[:, :, n_main:], 1, 2)
    bg_row = jnp.zeros((mlstm_b_gates.shape[0], 1, LANES), F32).at[:, 0, :n_gates].set(mlstm_b_gates)
    bg_col = mlstm_b_gates[:, :, None]
    gh = mlstm_norm_h[:, None, :]

    cos_p, sin_p = _rope_tables(jnp.arange(seq))
    cos_s, sin_s = _rope_tables(PAST_LEN + jnp.arange(t_new))
    cos_s, sin_s = jnp.tile(cos_s, (dec_batch, 1)), jnp.tile(sin_s, (dec_batch, 1))
    perms = [None] + [_perm_matrices(dil) for _, dil in DIL_PATTERNS[1:]]

    kv_p = [[] for _ in DIL_PATTERNS]
    kv_new = [[] for _ in DIL_PATTERNS]
    c_p, n_p, m_p, c_s, n_s, m_s = [], [], [], [], [], []
    conv_p, conv_s = [], []

    for layer in range(depth):
        if layer % 2 == 0:
            ia = layer // 2
            gains = jnp.stack([attn_q_norm[ia], attn_k_norm[ia]])
            h = _norm(xp, g_mix, layer, tm_p)
            os_, ls = [], []
            for g, (win, dil) in enumerate(DIL_PATTERNS):
                perm = None if perms[g] is None else perms[g][0]
                q, k, v, kvf = _group_proj(h, w_qkv, ia, g, gains, cos_p, sin_p, perm,
                                           ATTN_ROW_TILE, ROW_SUBTILE)
                o_g, l_g = _attn_prompt(q, k, v, g, batch, seq)
                os_.append(o_g)
                ls.append(l_g)
                keep = min(win, seq)
                kv_p[g].append(kvf.reshape((batch, seq, 2) + tile)[:, seq - keep:])
            xp = _combine_proj(os_, ls, [perms[1][1], perms[2][1]], w_o, ia, xp, ATTN_ROW_TILE, ROW_SUBTILE)
            h = _norm(xs, g_mix, layer, ms)
            qs, news = [], []
            for g in range(N_GROUPS):
                q, _, _, kvf = _group_proj(h, w_qkv, ia, g, gains, cos_s, sin_s, None, ms, ms)
                qs.append(q.astype(F32).reshape((dec_batch, t_new) + tile))
                news.append(kvf.reshape((dec_batch, t_new, 2) + tile))
                kv_new[g].append(news[-1])
            comb_s = _attn_sample(jnp.stack(qs), news, caches, ia, dec_batch, t_new)
            xs = _matmul_residual(comb_s.reshape(ms, ATT_W), w_o, ia, xs, ms)
        else:
            ib = layer // 2
            qkv, og, gates, gates_t = _mlstm_in_proj(xp, g_mix, layer, w_in, ib, wgt, bg_row, bg_col,
                                                     tm_p, dqk, dv)
            gated, c_f, n_f, m_f = _mlstm_core(qkv, og, gates, gates_t, gh, ib, batch, seq,
                                               MLSTM_CHUNK, dqk, dv, None)
            xp = _matmul_residual(gated.reshape(batch * seq, HEADS_B * dv), w_out, ib, xp, tm_p)
            c_p.append(c_f)
            n_p.append(n_f[:, :, 0, :])
            m_p.append(m_f[:, :, 0, 0])
            qkv, og, gates, _ = _mlstm_in_proj(xs, g_mix, layer, w_in, ib, wgt, bg_row, bg_col, ms, dqk, dv)
            tp = SAMPLE_PAD_T
            n_pad = tp - t_new

            def front_pad(a):
                a = a.reshape(dec_batch, t_new, a.shape[-1])
                a = jnp.pad(a, ((0, 0), (n_pad, 0), (0, 0)))
                return a.reshape(dec_batch * tp, a.shape[-1])

            lane_is_ig = (jnp.arange(LANES) < HEADS_B)[None, None, :]
            pad_gates = jnp.broadcast_to(jnp.where(lane_is_ig, NEG, 0.0).astype(F32), (dec_batch, n_pad, LANES))
            g3 = jnp.concatenate([pad_gates, gates.reshape(dec_batch, t_new, LANES)], axis=1)
            gates_t_pad = jnp.swapaxes(g3[:, :, :n_gates], 1, 2)
            state = (state_mlstm_C[ib], state_mlstm_n[ib][:, :, None, :],
                     jnp.broadcast_to(state_mlstm_m[ib][:, :, None, None], (dec_batch, HEADS_B, 1, LANES)))
            gated, c_f, n_f, m_f = _mlstm_core(front_pad(qkv), front_pad(og), g3.reshape(dec_batch * tp, LANES),
                                               gates_t_pad, gh, ib, dec_batch, tp, tp, dqk, dv, state)
            gated = gated[:, n_pad:].reshape(ms, HEADS_B * dv)
            xs = _matmul_residual(gated, w_out, ib, xs, ms)
            c_s.append(c_f)
            n_s.append(n_f[:, :, 0, :])
            m_s.append(m_f[:, :, 0, 0])

        xp, tails = _conv_ffn(xp, g_ffn, layer, w_up, ffn_conv_w, conv_b, w_down, tm_p, ROW_SUBTILE, seq=seq)
        tiles_per_seq = seq // tm_p
        tails = tails[tiles_per_seq - 1::tiles_per_seq, :, SUBLANES - (CONV_W - 1):, :]
        conv_p.append(jnp.swapaxes(tails, 1, 2).reshape(batch, CONV_W - 1, 2 * d_ff))
        st = state_ffn_conv[layer].reshape(dec_batch, CONV_W - 1, 2, d_ff)
        zeros = jnp.zeros((dec_batch, t_new, 2, d_ff), F32)

        def part_major(a):
            return jnp.transpose(a, (2, 0, 1, 3)).reshape(2, ms, d_ff)

        s1 = part_major(zeros.at[:, 0].set(st[:, 1]))
        s2 = part_major(zeros.at[:, 0].set(st[:, 0]).at[:, 1].set(st[:, 1]))
        xs, u_s = _conv_ffn(xs, g_ffn, layer, w_up, ffn_conv_w, conv_b, w_down, ms, ms,
                            sample_state=(s1, s2), t_new=t_new)
        u_s = jnp.moveaxis(u_s, 0, 1).reshape(dec_batch, t_new, 2 * d_ff)
        ext = jnp.concatenate([state_ffn_conv[layer], u_s], axis=1)
        conv_s.append(ext[:, t_new:])

    stack = jnp.stack
    kv_s = [jnp.concatenate([c[:, :, t_new:], stack(new)], axis=2) for c, new in zip(caches, kv_new)]
    return (xp.reshape(batch, seq, d), xs.reshape(dec_batch, t_new, d),
            stack(kv_p[0]), kv_s[0], stack(kv_p[1]), kv_s[1], stack(kv_p[2]), kv_s[2],
            stack(c_p), stack(c_s), stack(n_p), stack(n_s), stack(m_p), stack(m_s),
            stack(conv_p), stack(conv_s))
```

```python
import functools

import numpy as np
import jax
import jax.numpy as jnp
from jax import lax
from jax.experimental import pallas as pl
from jax.experimental.pallas import tpu as pltpu

F32 = jnp.float32
BF16 = jnp.bfloat16

DIL_PATTERNS = ((128, 1), (512, 4), (2048, 16))
N_GROUPS = len(DIL_PATTERNS)
HEADS_A = 8
HEAD_DIM = 128
ATT_W = HEADS_A * HEAD_DIM
BAND = 128
ROPE_THETA = 10000.0
PAST_LEN = 16384
HEADS_B = 8
GATE_CAP = 15.0
CONV_W = 3
EPS = 1e-6
NEG = -1e30

LANES = 128
SUBLANES = 8
VMEM_LIMIT_BYTES = 60 * 1024 * 1024
ROW_TILE = 1024
ATTN_ROW_TILE = 512
ROW_SUBTILE = 256
PERM_TILE = 256
PROJ_COL_TILE = 1024
FFN_COL_TILE = 512
FFN_ROW_SUBTILE = 512
MLSTM_CHUNK = 256
SAMPLE_PAD_T = 16


def _cparams(sem):
    return pltpu.CompilerParams(dimension_semantics=sem, vmem_limit_bytes=VMEM_LIMIT_BYTES)


def _rms_rows(x, g):
    ms = jnp.mean(x * x, axis=-1, keepdims=True)
    return (x * lax.rsqrt(ms + EPS)) * g


def _log_sigmoid(x):
    return jnp.minimum(x, 0.0) - jnp.log1p(jnp.exp(-jnp.abs(x)))


def _dot(a, b):
    return jnp.dot(a, b, preferred_element_type=F32)


def _dot_nt(a, b):
    return lax.dot_general(a, b, (((1,), (1,)), ((), ())), preferred_element_type=F32)


def _dot_tn(a, b):
    return lax.dot_general(a, b, (((0,), (0,)), ((), ())), preferred_element_type=F32)


def _sum(xs):
    return functools.reduce(lambda a, b: a + b, xs)


def _lane_fold(x, op):
    w = x.shape[-1]
    if w <= LANES or w % LANES:
        return x
    return functools.reduce(op, [x[:, i:i + LANES] for i in range(0, w, LANES)])


def _split3(x):
    hi = x.astype(BF16)
    r = x - hi.astype(F32)
    mid = r.astype(BF16)
    lo = (r - mid.astype(F32)).astype(BF16)
    return hi, mid, lo


def _select_rows(sel, x):
    return _sum([_dot(sel, p) for p in _split3(x)])


def _layer_spec(block, index_map, layer):
    return pl.BlockSpec((None,) + block, lambda *idx: (layer,) + index_map(*idx))


def _norm_kernel(x_ref, g_ref, h_ref):
    h_ref[...] = _rms_rows(x_ref[...], g_ref[...]).astype(BF16)


def _norm(x, gains, layer, tm):
    m, d = x.shape
    return pl.pallas_call(
        _norm_kernel,
        grid=(m // tm,),
        in_specs=[pl.BlockSpec((tm, d), lambda i: (i, 0)),
                  _layer_spec((1, d), lambda i: (0, 0), layer)],
        out_specs=pl.BlockSpec((tm, d), lambda i: (i, 0)),
        out_shape=jax.ShapeDtypeStruct((m, d), BF16),
        compiler_params=_cparams(("arbitrary",)),
        name="rms_norm",
    )(x, gains)


def _group_proj_kernel(*refs, rs, has_perm):
    h_ref, w_ref, gains_ref, cos_ref, sin_ref = refs[:5]
    rest = refs[5:]
    if has_perm:
        perm_ref, rest = rest[0], rest[1:]
    q_ref, k_ref, v_ref, kv_ref = rest
    tm = h_ref.shape[0]
    n_sub = tm // rs
    kv_rows = 2 * HEADS_A

    def head_cols(y, kind):
        return [y[:, kind * ATT_W + hh * HEAD_DIM:kind * ATT_W + (hh + 1) * HEAD_DIM] for hh in range(HEADS_A)]

    def rope(heads, gain, cos, sin):
        outs = []
        for y in heads:
            yn = _rms_rows(y, gain)
            outs.append(yn * cos + pltpu.roll(yn, HEAD_DIM // 2, 1) * sin)
        return outs

    def finish(r, y):
        rows = slice(r * rs, (r + 1) * rs)
        cos = cos_ref[rows, :]
        sin = sin_ref[rows, :]
        q = rope(head_cols(y, 0), gains_ref[0:1, :], cos, sin)
        k = rope(head_cols(y, 1), gains_ref[1:2, :], cos, sin)
        v = head_cols(y, 2)
        for kind, heads in enumerate((k, v)):
            for hh in range(HEADS_A):
                kv_ref[pl.ds(r * rs * kv_rows + kind * HEADS_A + hh, rs, stride=kv_rows), :] = heads[hh]
        for heads, ref in ((q, q_ref), (k, k_ref), (v, v_ref)):
            val = jnp.concatenate(heads, axis=1).astype(BF16)
            if has_perm:
                pt = perm_ref.shape[0]
                for c in range(rs // pt):
                    piece = _dot(perm_ref[...], val[c * pt:(c + 1) * pt, :]).astype(BF16)
                    ref[r * rs + c * pt:r * rs + (c + 1) * pt, :] = piece
            else:
                ref[rows, :] = val

    y = _dot(h_ref[0:rs, :], w_ref[...])
    for r in range(n_sub):
        y_next = _dot(h_ref[(r + 1) * rs:(r + 2) * rs, :], w_ref[...]) if r + 1 < n_sub else None
        finish(r, y)
        y = y_next


def _group_proj(h, w_all, layer, g, gains, cos, sin, perm, tm, rs):
    m, d = h.shape
    n_pos_tiles = cos.shape[0] // tm
    in_specs = [
        pl.BlockSpec((tm, d), lambda i: (i, 0)),
        _layer_spec((d, 3 * ATT_W), lambda i: (0, g), layer),
        pl.BlockSpec((2, HEAD_DIM), lambda i: (0, 0)),
        pl.BlockSpec((tm, HEAD_DIM), lambda i: (i % n_pos_tiles, 0)),
        pl.BlockSpec((tm, HEAD_DIM), lambda i: (i % n_pos_tiles, 0)),
    ]
    args = [h, w_all, gains, cos, sin]
    if perm is not None:
        in_specs.append(pl.BlockSpec(perm.shape, lambda i: (0, 0)))
        args.append(perm)
    row_spec = pl.BlockSpec((tm, ATT_W), lambda i: (i, 0))
    kv_rows = 2 * HEADS_A
    kernel = functools.partial(_group_proj_kernel, rs=rs, has_perm=perm is not None)
    return pl.pallas_call(
        kernel,
        grid=(m // tm,),
        in_specs=in_specs,
        out_specs=[row_spec, row_spec, row_spec, pl.BlockSpec((tm * kv_rows, HEAD_DIM), lambda i: (i, 0))],
        out_shape=[jax.ShapeDtypeStruct((m, ATT_W), BF16)] * 3
        + [jax.ShapeDtypeStruct((m * kv_rows, HEAD_DIM), F32)],
        compiler_params=_cparams(("arbitrary",)),
        name=f"attn_proj_g{g}",
    )(*args)


def _attn_prompt_kernel(*refs, tq, n_chunks, has_prev):
    q_ref, k_ref, v_ref = refs[:3]
    rest = refs[3:]
    if has_prev:
        kp_ref, vp_ref = rest[:2]
        rest = rest[2:]
    o_ref, lse_ref, qbuf, kbuf, vbuf, obuf, lbuf = rest
    n = pl.program_id(2)
    scale = HEAD_DIM ** -0.5
    cr = tq // n_chunks

    if has_prev:
        kbuf[0:BAND, :] = kp_ref[0, 0, 0]
        vbuf[0:BAND, :] = vp_ref[0, 0, 0]
    else:
        kbuf[0:BAND, :] = jnp.zeros((BAND, ATT_W), BF16)
        vbuf[0:BAND, :] = jnp.zeros((BAND, ATT_W), BF16)
    for t in range(n_chunks):
        qbuf[t * cr:(t + 1) * cr, :] = q_ref[0, t, 0]
        kbuf[BAND + t * cr:BAND + (t + 1) * cr, :] = k_ref[0, t, 0]
        vbuf[BAND + t * cr:BAND + (t + 1) * cr, :] = v_ref[0, t, 0]

    qi = lax.broadcasted_iota(jnp.int32, (BAND, 2 * BAND), 0)
    kj = lax.broadcasted_iota(jnp.int32, (BAND, 2 * BAND), 1)
    bias = jnp.where((kj >= qi) & (kj <= qi + BAND), 0.0, NEG).astype(F32)
    first_pen = jnp.where(n == 0, NEG, 0.0).astype(F32) if has_prev else NEG
    bias_first = bias + jnp.where(kj < BAND, first_pen, 0.0)
    lane = lax.broadcasted_iota(jnp.int32, (BAND, LANES), 1)

    for c in range(tq // BAND):
        rows = slice(c * BAND, (c + 1) * BAND)
        krows = slice(c * BAND, c * BAND + 2 * BAND)
        b = bias_first if c == 0 else bias
        lse_tile = jnp.zeros((BAND, LANES), F32)
        for h in range(HEADS_A):
            cols = slice(h * HEAD_DIM, (h + 1) * HEAD_DIM)
            s = _dot_nt(qbuf[rows, cols], kbuf[krows, cols]) * scale + b
            mx = jnp.max(s, axis=-1, keepdims=True)
            p = jnp.exp(s - mx)
            den = jnp.sum(p, axis=-1, keepdims=True)
            obuf[rows, cols] = _dot(p.astype(BF16), vbuf[krows, cols]) / den
            lse_tile = jnp.where(lane == h, mx + jnp.log(den), lse_tile)
        lbuf[rows, :] = lse_tile

    for t in range(n_chunks):
        o_ref[0, t, 0] = obuf[t * cr:(t + 1) * cr, :]
        lse_ref[0, t, 0] = lbuf[t * cr:(t + 1) * cr, :]


def _attn_prompt(q, k, v, g, batch, seq):
    _, dil = DIL_PATTERNS[g]
    l = seq // dil
    if dil == 1:
        tq, n_chunks = min(l, 512), 1
        view = (batch, l // tq, 1, tq)
        block = (1, 1, 1, tq)
        index = lambda b, r, n: (b, n, 0, 0, 0)
    else:
        tq, n_chunks = l, seq // PERM_TILE
        view = (batch, n_chunks, dil, PERM_TILE // dil)
        block = (1, n_chunks, 1, PERM_TILE // dil)
        index = lambda b, r, n: (b, 0, r, 0, 0)
    nq = l // tq
    has_prev = nq > 1

    def spec(width):
        return pl.BlockSpec(block + (width,), index)

    in_specs = [spec(ATT_W)] * 3
    args = [a.reshape(view + (ATT_W,)) for a in (q, k, v)]
    if has_prev:
        sub = tq // BAND
        prev_spec = pl.BlockSpec((1, 1, 1, BAND, ATT_W),
                                 lambda b, r, n: (b, jnp.maximum(n * sub - 1, 0), 0, 0, 0))
        in_specs += [prev_spec] * 2
        args += [a.reshape(batch, l // BAND, 1, BAND, ATT_W) for a in (k, v)]
    kernel = functools.partial(_attn_prompt_kernel, tq=tq, n_chunks=n_chunks, has_prev=has_prev)
    o, lse = pl.pallas_call(
        kernel,
        grid=(batch, dil, nq),
        in_specs=in_specs,
        out_specs=[spec(ATT_W), spec(LANES)],
        out_shape=[jax.ShapeDtypeStruct(view + (ATT_W,), F32), jax.ShapeDtypeStruct(view + (LANES,), F32)],
        scratch_shapes=[pltpu.VMEM((tq, ATT_W), BF16), pltpu.VMEM((BAND + tq, ATT_W), BF16),
                        pltpu.VMEM((BAND + tq, ATT_W), BF16), pltpu.VMEM((tq, ATT_W), F32),
                        pltpu.VMEM((tq, LANES), F32)],
        compiler_params=_cparams(("arbitrary", "arbitrary", "arbitrary")),
        name=f"attn_prompt_g{g}",
    )(*args)
    return o.reshape(batch * seq, ATT_W), lse.reshape(batch * seq, LANES)


def _combine_proj_kernel(o0_ref, l0_ref, o1_ref, l1_ref, o2_ref, l2_ref, s1_ref, s2_ref,
                         w_ref, x_ref, out_ref, *, rs):
    tm = x_ref.shape[0]
    for r in range(tm // rs):
        rows = slice(r * rs, (r + 1) * rs)
        os_ = [o0_ref[rows, :], _select_rows(s1_ref[...], o1_ref[rows, :]),
               _select_rows(s2_ref[...], o2_ref[rows, :])]
        ls = [l0_ref[rows, :], _select_rows(s1_ref[...], l1_ref[rows, :]),
              _select_rows(s2_ref[...], l2_ref[rows, :])]
        top = functools.reduce(jnp.maximum, ls)
        es = [jnp.exp(l - top) for l in ls]
        tot = _sum(es)
        alphas = [e / tot for e in es]
        heads = []
        for h in range(HEADS_A):
            cols = slice(h * HEAD_DIM, (h + 1) * HEAD_DIM)
            heads.append(_sum([a[:, h:h + 1] * o[:, cols] for a, o in zip(alphas, os_)]))
        comb = jnp.concatenate(heads, axis=1).astype(BF16)
        out_ref[rows, :] = x_ref[rows, :] + _dot(comb, w_ref[...])


def _combine_proj(os_, ls, sels, w_all, layer, x, tm, rs):
    m, d = x.shape
    o_spec = pl.BlockSpec((tm, ATT_W), lambda i: (i, 0))
    l_spec = pl.BlockSpec((tm, LANES), lambda i: (i, 0))
    sel_spec = pl.BlockSpec((rs, rs), lambda i: (0, 0))
    x_spec = pl.BlockSpec((tm, d), lambda i: (i, 0))
    kernel = functools.partial(_combine_proj_kernel, rs=rs)
    return pl.pallas_call(
        kernel,
        grid=(m // tm,),
        in_specs=[o_spec, l_spec, o_spec, l_spec, o_spec, l_spec, sel_spec, sel_spec,
                  _layer_spec((ATT_W, d), lambda i: (0, 0), layer), x_spec],
        out_specs=x_spec,
        out_shape=jax.ShapeDtypeStruct((m, d), F32),
        compiler_params=_cparams(("arbitrary",)),
        name="attn_combine_proj",
    )(os_[0], ls[0], os_[1], ls[1], os_[2], ls[2], sels[0], sels[1], w_all, x)


def _attn_sample_kernel(q_ref, kvn0_ref, kvn1_ref, kvn2_ref, c0_ref, c1_ref, c2_ref, o_ref, *, t_new):
    scale = HEAD_DIM ** -0.5
    caches = (c0_ref, c1_ref, c2_ref)
    news = (kvn0_ref, kvn1_ref, kvn2_ref)
    key_u = lax.broadcasted_iota(jnp.int32, (BAND, HEADS_A, 1), 0)

    for t in range(t_new):
        o_g, lse_g = [], []
        for g, (_, dil) in enumerate(DIL_PATTERNS):
            q = q_ref[g, 0, t]
            res = 0 if dil == 1 else t
            s_c = jnp.sum(caches[g][:, res, 0] * q[None], axis=-1, keepdims=True) * scale
            if dil == 1:
                s_c = jnp.where(key_u >= t, s_c, NEG)
                new_ts = list(range(t + 1))
            else:
                new_ts = [t]
            s_n = [jnp.sum(q * news[g][0, t2, 0], axis=-1, keepdims=True) * scale for t2 in new_ts]
            mx = functools.reduce(jnp.maximum, s_n, jnp.max(s_c, axis=0))
            p_c = jnp.exp(s_c - mx[None])
            den = jnp.sum(p_c, axis=0)
            acc = jnp.sum(p_c * caches[g][:, res, 1], axis=0)
            for t2, s in zip(new_ts, s_n):
                p_n = jnp.exp(s - mx)
                den = den + p_n
                acc = acc + p_n * news[g][0, t2, 1]
            o_g.append(acc / den)
            lse_g.append(mx + jnp.log(den))
        top = functools.reduce(jnp.maximum, lse_g)
        es = [jnp.exp(l - top) for l in lse_g]
        tot = _sum(es)
        o_ref[0, t] = _sum([(e / tot) * o for e, o in zip(es, o_g)])


def _attn_sample(q, kv_new, caches, layer, batch, t_new):
    tile = (HEADS_A, HEAD_DIM)
    in_specs = [pl.BlockSpec((N_GROUPS, 1, t_new) + tile, lambda b: (0, b, 0, 0, 0))]
    in_specs += [pl.BlockSpec((1, t_new, 2) + tile, lambda b: (b, 0, 0, 0, 0))] * N_GROUPS
    args = [q] + list(kv_new)
    for (win, dil), c in zip(DIL_PATTERNS, caches):
        n_res = min(dil, t_new)
        in_specs.append(pl.BlockSpec((None, None, BAND, n_res, 2) + tile,
                                     lambda b: (layer, b, 0, 0, 0, 0, 0)))
        args.append(c.reshape(c.shape[:2] + (BAND, dil, 2) + tile))
    kernel = functools.partial(_attn_sample_kernel, t_new=t_new)
    return pl.pallas_call(
        kernel,
        grid=(batch,),
        in_specs=in_specs,
        out_specs=pl.BlockSpec((1, t_new) + tile, lambda b: (b, 0, 0, 0)),
        out_shape=jax.ShapeDtypeStruct((batch, t_new) + tile, F32),
        compiler_params=_cparams(("arbitrary",)),
        name="attn_sample",
    )(*args)


def _matmul_residual_kernel(a_ref, w_ref, x_ref, o_ref):
    o_ref[...] = x_ref[...] + _dot(a_ref[...].astype(BF16), w_ref[...])


def _matmul_residual(a, w_all, layer, x, tm):
    m, k = a.shape
    n = w_all.shape[-1]
    tn = PROJ_COL_TILE
    return pl.pallas_call(
        _matmul_residual_kernel,
        grid=(m // tm, n // tn),
        in_specs=[
            pl.BlockSpec((tm, k), lambda i, j: (i, 0)),
            _layer_spec((k, tn), lambda i, j: (0, j), layer),
            pl.BlockSpec((tm, tn), lambda i, j: (i, j)),
        ],
        out_specs=pl.BlockSpec((tm, tn), lambda i, j: (i, j)),
        out_shape=jax.ShapeDtypeStruct((m, n), F32),
        compiler_params=_cparams(("arbitrary", "arbitrary")),
        name="matmul_residual",
    )(a, w_all, x)


def _gate_act(pre, is_input_gate):
    capped = GATE_CAP * jnp.tanh(pre / GATE_CAP)
    return jnp.where(is_input_gate, capped, _log_sigmoid(capped))


def _mlstm_in_kernel(x_ref, g_ref, w_ref, wg_ref, wgt_ref, bg_ref, bgt_ref,
                     qkv_ref, og_ref, gates_ref, gatest_ref, h_scr, *, k_tiles, n_qkv_tiles, k_scale):
    j = pl.program_id(1)

    @pl.when(j == 0)
    def _():
        h = _rms_rows(x_ref[...], g_ref[...]).astype(BF16)
        h_scr[...] = h
        n_gates = wgt_ref.shape[0]
        pre = _dot(h, wg_ref[...]) + bg_ref[...]
        lane = lax.broadcasted_iota(jnp.int32, pre.shape, 1)
        gates_ref[...] = jnp.where(lane < n_gates, _gate_act(pre, lane < n_gates // 2), 0.0)
        pre_t = _dot_nt(wgt_ref[...], h) + bgt_ref[...]
        row = lax.broadcasted_iota(jnp.int32, pre_t.shape, 0)
        gatest_ref[...] = _gate_act(pre_t, row < n_gates // 2)

    y = _dot(h_scr[...], w_ref[...])

    @pl.when(j < n_qkv_tiles)
    def _():
        is_k = (j >= k_tiles[0]) & (j < k_tiles[1])
        qkv_ref[...] = (y * jnp.where(is_k, k_scale, 1.0).astype(F32)).astype(BF16)

    @pl.when(j >= n_qkv_tiles)
    def _():
        og_ref[...] = y


def _mlstm_in_proj(x, gains, layer, w_all, ib, wgt_all, bg_all, bgt_all, tm, dqk, dv):
    m, d = x.shape
    tn = PROJ_COL_TILE
    qk_w, v_w = HEADS_B * dqk, HEADS_B * dv
    n_main = 2 * qk_w + 2 * v_w
    n_gates = w_all.shape[-1] - n_main
    n_qkv_tiles = (2 * qk_w + v_w) // tn
    nj = n_main // tn
    kernel = functools.partial(_mlstm_in_kernel, k_tiles=(qk_w // tn, 2 * qk_w // tn),
                               n_qkv_tiles=n_qkv_tiles, k_scale=dqk ** -0.5)
    return pl.pallas_call(
        kernel,
        grid=(m // tm, nj),
        in_specs=[
            pl.BlockSpec((tm, d), lambda i, j: (i, 0)),
            _layer_spec((1, d), lambda i, j: (0, 0), layer),
            _layer_spec((d, tn), lambda i, j: (0, j), ib),
            _layer_spec((d, LANES), lambda i, j: (0, n_main // LANES), ib),
            _layer_spec((n_gates, d), lambda i, j: (0, 0), ib),
            _layer_spec((1, LANES), lambda i, j: (0, 0), ib),
            _layer_spec((n_gates, 1), lambda i, j: (0, 0), ib),
        ],
        out_specs=[
            pl.BlockSpec((tm, tn), lambda i, j: (i, jnp.minimum(j, n_qkv_tiles - 1))),
            pl.BlockSpec((tm, tn), lambda i, j: (i, jnp.maximum(j - n_qkv_tiles, 0))),
            pl.BlockSpec((tm, LANES), lambda i, j: (i, 0)),
            pl.BlockSpec((n_gates, tm), lambda i, j: (0, i)),
        ],
        out_shape=[
            jax.ShapeDtypeStruct((m, 2 * qk_w + v_w), BF16),
            jax.ShapeDtypeStruct((m, v_w), F32),
            jax.ShapeDtypeStruct((m, LANES), F32),
            jax.ShapeDtypeStruct((n_gates, m), F32),
        ],
        scratch_shapes=[pltpu.VMEM((tm, d), BF16)],
        compiler_params=_cparams(("arbitrary", "arbitrary")),
        name="mlstm_in_proj",
    )(x, gains, w_all, w_all, wgt_all, bg_all, bgt_all)


def _mlstm_core_kernel(*refs, chunk, dqk, dv, has_state):
    q_ref, k_ref, v_ref, og_ref, gates_ref, gatest_ref, gh_ref = refs[:7]
    rest = refs[7:]
    if has_state:
        c0_ref, n0_ref, m0_ref = rest[:3]
        rest = rest[3:]
    out_ref, c_out, n_out, m_out, c_scr, n_scr, m_scr = rest
    c = pl.program_id(1)
    n_heads = c_scr.shape[0]

    @pl.when(c == 0)
    def _():
        if has_state:
            c_scr[...] = c0_ref[0]
            n_scr[...] = n0_ref[0]
            m_scr[...] = m0_ref[0]
        else:
            c_scr[...] = jnp.zeros_like(c_scr)
            n_scr[...] = jnp.zeros_like(n_scr)
            m_scr[...] = jnp.zeros_like(m_scr)

    row = lax.broadcasted_iota(jnp.int32, (chunk, chunk), 0)
    col = lax.broadcasted_iota(jnp.int32, (chunk, chunk), 1)
    causal = col <= row
    tri_lo = jnp.where(causal, 1.0, 0.0).astype(BF16)
    tri_up = jnp.where(row <= col, 1.0, 0.0).astype(BF16)

    gates = gates_ref[0]
    gates_t = gatest_ref[0] if len(gatest_ref.shape) == 3 else gatest_ref[...]
    b_cols = _sum([_dot(tri_lo, p) for p in _split3(gates)])
    b_rows = _sum([_dot(p, tri_up) for p in _split3(gates_t)])

    heads = range(n_heads)
    q = [q_ref[0, :, hh * dqk:(hh + 1) * dqk] for hh in heads]
    k = [k_ref[0, :, hh * dqk:(hh + 1) * dqk] for hh in heads]
    v = [v_ref[0, :, hh * dv:(hh + 1) * dv] for hh in heads]
    ig_col = [gates[:, hh:hh + 1] for hh in heads]
    b_col = [b_cols[:, n_heads + hh:n_heads + hh + 1] for hh in heads]
    m_prev = [m_scr[hh][:, 0:1] for hh in heads]

    inter, m_t, pexp = [], [], []
    for hh in heads:
        a_row = gates_t[hh:hh + 1, :] - b_rows[n_heads + hh:n_heads + hh + 1, :]
        a_mask = jnp.where(causal, a_row, NEG)
        a_max = jnp.max(_lane_fold(a_mask, jnp.maximum), axis=-1, keepdims=True)
        inter.append(b_col[hh] + m_prev[hh])
        m_t.append(jnp.maximum(inter[hh], b_col[hh] + a_max))
        pexp.append(jnp.exp(a_mask + (b_col[hh] - m_t[hh])))
    s = [_dot_nt(q[hh], k[hh]) for hh in heads]
    qc = [_dot_nt(q[hh], c_scr[hh].astype(BF16)) for hh in heads]
    p = [pexp[hh] * s[hh] for hh in heads]
    pv = [_dot(p[hh].astype(BF16), v[hh]) for hh in heads]
    for hh in heads:
        w_inter = jnp.exp(inter[hh] - m_t[hh])
        num = w_inter * qc[hh] + pv[hh]
        nq = jnp.sum(q[hh].astype(F32) * n_scr[hh], axis=-1, keepdims=True)
        den = w_inter * nq + jnp.sum(_lane_fold(p[hh], jnp.add), axis=-1, keepdims=True)
        r = 1.0 / jnp.maximum(jnp.abs(den), jnp.exp(-m_t[hh]))
        ms_num = jnp.sum(_lane_fold(num * num, jnp.add), axis=-1, keepdims=True) / dv
        scale = r * lax.rsqrt(r * r * ms_num + EPS)
        cols = slice(hh * dv, (hh + 1) * dv)
        gated = num * scale * gh_ref[:, cols] * jax.nn.sigmoid(og_ref[0, :, cols])
        out_ref[0, :, cols] = gated.astype(BF16)

    w_c, vw, w_s = [], [], []
    for hh in heads:
        b_last = b_col[hh][chunk - 1:chunk, :]
        dec = b_last - b_col[hh] + ig_col[hh]
        m_new = jnp.maximum(b_last + m_prev[hh], jnp.max(dec, axis=0, keepdims=True))
        w_c.append(jnp.exp(b_last + m_prev[hh] - m_new))
        w_s.append(jnp.exp(dec - m_new))
        vw.append((v[hh].astype(F32) * w_s[hh]).astype(BF16))
        m_scr[hh] = jnp.broadcast_to(m_new, (1, LANES))
    upd = [_dot_tn(vw[hh], k[hh]) for hh in heads]
    for hh in heads:
        c_scr[hh] = w_c[hh] * c_scr[hh] + upd[hh]
        n_scr[hh] = w_c[hh] * n_scr[hh] + jnp.sum(k[hh].astype(F32) * w_s[hh], axis=0, keepdims=True)

    @pl.when(c == pl.num_programs(1) - 1)
    def _():
        c_out[0] = c_scr[...]
        n_out[0] = n_scr[...]
        m_out[0] = m_scr[...]


def _mlstm_core(qkv, og, gates, gates_t, gh_all, ib, batch, seq, chunk, dqk, dv, state):
    n_heads = HEADS_B
    qk_w, v_w = n_heads * dqk, n_heads * dv
    nc = seq // chunk
    qkv3 = qkv.reshape(batch, seq, 2 * qk_w + v_w)
    state_specs = [
        pl.BlockSpec((1, n_heads, dv, dqk), lambda b, c: (b, 0, 0, 0)),
        pl.BlockSpec((1, n_heads, 1, dqk), lambda b, c: (b, 0, 0, 0)),
        pl.BlockSpec((1, n_heads, 1, LANES), lambda b, c: (b, 0, 0, 0)),
    ]
    in_specs = [
        pl.BlockSpec((1, chunk, qk_w), lambda b, c: (b, c, 0)),
        pl.BlockSpec((1, chunk, qk_w), lambda b, c: (b, c, 1)),
        pl.BlockSpec((1, chunk, v_w), lambda b, c: (b, c, 2 * qk_w // v_w)),
        pl.BlockSpec((1, chunk, v_w), lambda b, c: (b, c, 0)),
        pl.BlockSpec((1, chunk, LANES), lambda b, c: (b, c, 0)),
        pl.BlockSpec((2 * n_heads, chunk), lambda b, c: (0, b * nc + c)) if gates_t.ndim == 2
        else pl.BlockSpec((1, 2 * n_heads, chunk), lambda b, c: (b * nc + c, 0, 0)),
        _layer_spec((1, v_w), lambda b, c: (0, 0), ib),
    ]
    args = [qkv3, qkv3, qkv3, og.reshape(batch, seq, v_w), gates.reshape(batch, seq, LANES), gates_t, gh_all]
    if state is not None:
        in_specs += state_specs
        args += list(state)
    kernel = functools.partial(_mlstm_core_kernel, chunk=chunk, dqk=dqk, dv=dv,
                               has_state=state is not None)
    return pl.pallas_call(
        kernel,
        grid=(batch, nc),
        in_specs=in_specs,
        out_specs=[pl.BlockSpec((1, chunk, v_w), lambda b, c: (b, c, 0))] + state_specs,
        out_shape=[
            jax.ShapeDtypeStruct((batch, seq, v_w), BF16),
            jax.ShapeDtypeStruct((batch, n_heads, dv, dqk), F32),
            jax.ShapeDtypeStruct((batch, n_heads, 1, dqk), F32),
            jax.ShapeDtypeStruct((batch, n_heads, 1, LANES), F32),
        ],
        scratch_shapes=[pltpu.VMEM((n_heads, dv, dqk), F32), pltpu.VMEM((n_heads, 1, dqk), F32),
                        pltpu.VMEM((n_heads, 1, LANES), F32)],
        compiler_params=_cparams(("arbitrary", "arbitrary")),
        name="mlstm_core",
    )(*args)


def _background_copies(src_refs, out_refs, jobs, sems):
    return [pltpu.make_async_copy(src_refs[s] if s_idx is None else src_refs[s].at[s_idx],
                                  out_refs[o].at[o_idx], sems.at[n])
            for n, (s, s_idx, o, o_idx) in enumerate(jobs)]


def _ffn_kernel(*refs, rs, tiles_per_seq, t_new, n_bg_in, n_bg_out, bg_jobs):
    x_ref, g_ref, wa_ref, wb_ref, cwa_ref, cwb_ref, cba_ref, cbb_ref, wd_ref = refs[:9]
    rest = refs[9:]
    sample = t_new is not None
    if sample:
        s1_ref, s2_ref, o_ref, u_ref, h_scr = rest
    else:
        bg_in, rest = rest[:n_bg_in], rest[n_bg_in:]
        o_ref, tail_ref = rest[:2]
        bg_out, rest = rest[2:2 + n_bg_out], rest[2 + n_bg_out:]
        h_scr, tail_scr = rest[:2]
    i = pl.program_id(0)
    f = pl.program_id(1)
    tm = x_ref.shape[0]
    n_sub = tm // rs

    if not sample and bg_jobs:
        copies = _background_copies(bg_in, bg_out, bg_jobs, rest[2])

        @pl.when((i == 0) & (f == 0))
        def _():
            for cp in copies:
                cp.start()

        @pl.when((i == pl.num_programs(0) - 1) & (f == pl.num_programs(1) - 1))
        def _():
            for cp in copies:
                cp.wait()

    @pl.when(f == 0)
    def _():
        x = x_ref[...]
        h_scr[...] = _rms_rows(x, g_ref[...]).astype(BF16)
        o_ref[...] = x

    parts = ((wa_ref, cwa_ref, cba_ref), (wb_ref, cwb_ref, cbb_ref))
    if not sample:
        @pl.when((i == 0) & (f == 0))
        def _():
            tail_scr[...] = jnp.zeros_like(tail_scr)

        seq_start = i % tiles_per_seq == 0
        prev = []
        for part in range(2):
            t = jnp.where(seq_start, 0.0, tail_scr[f, part])
            prev.append((t[SUBLANES - 2:SUBLANES - 1, :], t[SUBLANES - 1:SUBLANES, :]))

    def up(r):
        h = h_scr[r * rs:(r + 1) * rs, :]
        return [_dot(h, w_ref[...]) for w_ref, _, _ in parts]

    def conv_gate(r, us):
        conv = []
        for part, (_, cw_ref, cb_ref) in enumerate(parts):
            u = us[part]
            ridx = lax.broadcasted_iota(jnp.int32, u.shape, 0)
            r1 = pltpu.roll(u, 1, 0)
            r2 = pltpu.roll(u, 2, 0)
            if sample:
                u_ref[part] = u
                pos = ridx % t_new
                u1 = jnp.where(pos >= 1, r1, s1_ref[part])
                u2 = jnp.where(pos >= 2, r2, s2_ref[part])
            else:
                p2, p1 = prev[part]
                u1 = jnp.where(ridx == 0, p1, r1)
                u2 = jnp.where(ridx == 0, p2, jnp.where(ridx == 1, p1, r2))
                prev[part] = (u[rs - 2:rs - 1, :], u[rs - 1:rs, :])
                if r == n_sub - 1:
                    tail = u[rs - SUBLANES:rs, :]
                    tail_scr[f, part] = tail
                    tail_ref[0, part] = tail
            cw = cw_ref[...]
            conv.append(cb_ref[...] + cw[0:1, :] * u2 + cw[1:2, :] * u1 + cw[2:3, :] * u)
        return (jax.nn.silu(conv[0]) * conv[1]).astype(BF16)

    us = up(0)
    for r in range(n_sub):
        us_next = up(r + 1) if r + 1 < n_sub else None
        z = conv_gate(r, us)
        o_ref[r * rs:(r + 1) * rs, :] += _dot(z, wd_ref[...])
        us = us_next


def _conv_ffn(x, gains, layer, w_up_all, conv_w_all, conv_b_all, w_down_all, tm, rs,
              seq=None, sample_state=None, t_new=None, background=None):
    m, d = x.shape
    d_ff = w_down_all.shape[1]
    tf = FFN_COL_TILE
    nf = d_ff // tf
    sample = sample_state is not None
    in_specs = [
        pl.BlockSpec((tm, d), lambda i, f: (i, 0)),
        _layer_spec((1, d), lambda i, f: (0, 0), layer),
        _layer_spec((d, tf), lambda i, f: (0, f), layer),
        _layer_spec((d, tf), lambda i, f: (0, nf + f), layer),
        _layer_spec((CONV_W, tf), lambda i, f: (0, f), layer),
        _layer_spec((CONV_W, tf), lambda i, f: (0, nf + f), layer),
        _layer_spec((1, tf), lambda i, f: (0, f), layer),
        _layer_spec((1, tf), lambda i, f: (0, nf + f), layer),
        _layer_spec((tf, d), lambda i, f: (f, 0), layer),
    ]
    args = [x, gains, w_up_all, w_up_all, conv_w_all, conv_w_all, conv_b_all, conv_b_all, w_down_all]
    scratch = [pltpu.VMEM((tm, d), BF16)]
    if sample:
        tiles_per_seq = None
        in_specs += [pl.BlockSpec((2, tm, tf), lambda i, f: (0, i, f))] * 2
        args += list(sample_state)
        out_specs = [pl.BlockSpec((tm, d), lambda i, f: (i, 0)),
                     pl.BlockSpec((2, tm, tf), lambda i, f: (0, i, f))]
        out_shape = [jax.ShapeDtypeStruct((m, d), F32), jax.ShapeDtypeStruct((2, m, d_ff), F32)]
    else:
        tiles_per_seq = seq // tm
        out_specs = [pl.BlockSpec((tm, d), lambda i, f: (i, 0)),
                     pl.BlockSpec((1, 2, SUBLANES, tf), lambda i, f: (i, 0, 0, f))]
        out_shape = [jax.ShapeDtypeStruct((m, d), F32),
                     jax.ShapeDtypeStruct((m // tm, 2, SUBLANES, d_ff), F32)]
        scratch.append(pltpu.VMEM((nf, 2, SUBLANES, tf), F32))
    bg_srcs, bg_outs, bg_jobs = background if background is not None else ((), (), ())
    if bg_jobs:
        any_spec = pl.BlockSpec(memory_space=pl.ANY)
        in_specs += [any_spec] * len(bg_srcs)
        args += list(bg_srcs)
        out_specs += [any_spec] * len(bg_outs)
        out_shape += list(bg_outs)
        scratch.append(pltpu.SemaphoreType.DMA((len(bg_jobs),)))
    kernel = functools.partial(_ffn_kernel, rs=rs, tiles_per_seq=tiles_per_seq, t_new=t_new,
                               n_bg_in=len(bg_srcs), n_bg_out=len(bg_outs), bg_jobs=tuple(bg_jobs))
    return pl.pallas_call(
        kernel,
        grid=(m // tm, nf),
        in_specs=in_specs,
        out_specs=out_specs,
        out_shape=out_shape,
        scratch_shapes=scratch,
        compiler_params=_cparams(("arbitrary", "arbitrary")),
        name="conv_ffn_sample" if sample else "conv_ffn",
    )(*args)


def _cache_shift_jobs(caches, new_rows):
    jobs = []
    for g, (cache, new) in enumerate(zip(caches, new_rows)):
        n_buf, n_new = cache.shape[2], new.shape[2]
        jobs.append((g, (slice(None), slice(None), pl.ds(n_new, n_buf - n_new)),
                     g, (slice(None), slice(None), pl.ds(0, n_buf - n_new))))
        jobs.append((len(caches) + g, None, g, (slice(None), slice(None), pl.ds(n_buf - n_new, n_new))))
    outs = [jax.ShapeDtypeStruct(c.shape, c.dtype) for c in caches]
    return list(caches) + list(new_rows), outs, jobs


def _kv_stack_jobs(kv_layers, seq):
    srcs, outs, jobs = [], [], []
    for g, (win, _) in enumerate(DIL_PATTERNS):
        batch, rows_all, dim = kv_layers[g][0].shape
        rows_per_pos = rows_all // seq
        keep = min(win, seq) * rows_per_pos
        outs.append(jax.ShapeDtypeStruct((len(kv_layers[g]), batch, keep, dim), F32))
        for layer, a in enumerate(kv_layers[g]):
            jobs.append((len(srcs), (slice(None), pl.ds(rows_all - keep, keep)), g, layer))
            srcs.append(a)
    return srcs, outs, jobs


def _rope_tables(pos):
    half = HEAD_DIM // 2
    inv_freq = ROPE_THETA ** (-jnp.arange(half, dtype=F32) / half)
    ang = pos.astype(F32)[:, None] * inv_freq[None, :]
    cos, sin = jnp.cos(ang), jnp.sin(ang)
    return jnp.concatenate([cos, cos], axis=-1), jnp.concatenate([-sin, sin], axis=-1)


def _perm_matrices(dil):
    pos = np.arange(PERM_TILE)
    slot = (pos % dil) * (PERM_TILE // dil) + pos // dil
    to_residue_major = np.zeros((PERM_TILE, PERM_TILE), np.float32)
    to_residue_major[slot, pos] = 1.0
    return jnp.asarray(to_residue_major, BF16), jnp.asarray(to_residue_major.T, BF16)


def kernel(x_prompt, x_sample, cache_kv_w128, cache_kv_w512, cache_kv_w2048, state_mlstm_C, state_mlstm_n, state_mlstm_m, state_ffn_conv, norm_mix, norm_ffn, attn_w_qkv, attn_q_norm, attn_k_norm, attn_w_o, mlstm_w_in, mlstm_b_gates, mlstm_norm_h, mlstm_w_out, ffn_w_up, ffn_conv_w, ffn_conv_b, ffn_w_down):
    batch, seq, d = x_prompt.shape
    dec_batch, t_new, _ = x_sample.shape
    depth = norm_mix.shape[0]
    caches = (cache_kv_w128, cache_kv_w512, cache_kv_w2048)
    dqk = state_mlstm_C.shape[-1]
    dv = state_mlstm_C.shape[-2]
    d_ff = ffn_w_down.shape[1]
    ms = dec_batch * t_new
    tm_p = ROW_TILE
    tile = (HEADS_A, HEAD_DIM)

    xp = x_prompt.reshape(batch * seq, d)
    xs = x_sample.reshape(ms, d)

    w_qkv, w_o, w_in, w_out, w_up, w_down = (a.astype(BF16) for a in (
        attn_w_qkv, attn_w_o, mlstm_w_in, mlstm_w_out, ffn_w_up, ffn_w_down))
    g_mix = norm_mix[:, None, :]
    g_ffn = norm_ffn[:, None, :]
    conv_b = ffn_conv_b[:, None, :]
    n_main = 2 * HEADS_B * dqk + 2 * HEADS_B * dv
    n_gates = mlstm_b_gates.shape[-1]
    wgt = jnp.swapaxes(mlstm_w_in[:, :, n_main:], 1, 2).astype(BF16)
    bg_row = jnp.zeros((mlstm_b_gates.shape[0], 1, LANES), F32).at[:, 0, :n_gates].set(mlstm_b_gates)
    bg_col = mlstm_b_gates[:, :, None]
    gh = mlstm_norm_h[:, None, :]

    cos_p, sin_p = _rope_tables(jnp.arange(seq))
    cos_s, sin_s = _rope_tables(PAST_LEN + jnp.arange(t_new))
    cos_s, sin_s = jnp.tile(cos_s, (dec_batch, 1)), jnp.tile(sin_s, (dec_batch, 1))
    perms = [None] + [_perm_matrices(dil) for _, dil in DIL_PATTERNS[1:]]

    kv_p = [[] for _ in DIL_PATTERNS]
    assert depth % 2 == 0, "the background copies ride on the last two FFN layers"
    kv_new = [[] for _ in DIL_PATTERNS]
    c_p, n_p, m_p, c_s, n_s, m_s = [], [], [], [], [], []
    conv_p, conv_s = [], []

    for layer in range(depth):
        if layer % 2 == 0:
            ia = layer // 2
            gains = jnp.stack([attn_q_norm[ia], attn_k_norm[ia]])
            h = _norm(xp, g_mix, layer, tm_p)
            os_, ls = [], []
            for g, (win, dil) in enumerate(DIL_PATTERNS):
                perm = None if perms[g] is None else perms[g][0]
                q, k, v, kvf = _group_proj(h, w_qkv, ia, g, gains, cos_p, sin_p, perm,
                                           ATTN_ROW_TILE, ATTN_ROW_TILE)
                kv_p[g].append(kvf.reshape(batch, seq * 2 * HEADS_A, HEAD_DIM))
                o_g, l_g = _attn_prompt(q, k, v, g, batch, seq)
                os_.append(o_g)
                ls.append(l_g)
            xp = _combine_proj(os_, ls, [perms[1][1], perms[2][1]], w_o, ia, xp, ATTN_ROW_TILE, ROW_SUBTILE)
            h = _norm(xs, g_mix, layer, ms)
            qs, news = [], []
            for g in range(N_GROUPS):
                q, _, _, kvf = _group_proj(h, w_qkv, ia, g, gains, cos_s, sin_s, None, ms, ms)
                qs.append(q.astype(F32).reshape((dec_batch, t_new) + tile))
                news.append(kvf.reshape((dec_batch, t_new, 2) + tile))
                kv_new[g].append(news[-1])
            comb_s = _attn_sample(jnp.stack(qs), news, caches, ia, dec_batch, t_new)
            xs = _matmul_residual(comb_s.reshape(ms, ATT_W), w_o, ia, xs, ms)
        else:
            ib = layer // 2
            qkv, og, gates, gates_t = _mlstm_in_proj(xp, g_mix, layer, w_in, ib, wgt, bg_row, bg_col,
                                                     tm_p, dqk, dv)
            gated, c_f, n_f, m_f = _mlstm_core(qkv, og, gates, gates_t, gh, ib, batch, seq,
                                               MLSTM_CHUNK, dqk, dv, None)
            xp = _matmul_residual(gated.reshape(batch * seq, HEADS_B * dv), w_out, ib, xp, tm_p)
            c_p.append(c_f)
            n_p.append(n_f[:, :, 0, :])
            m_p.append(m_f[:, :, 0, 0])
            qkv, og, gates, _ = _mlstm_in_proj(xs, g_mix, layer, w_in, ib, wgt, bg_row, bg_col, ms, dqk, dv)
            tp = SAMPLE_PAD_T
            n_pad = tp - t_new

            def front_pad(a):
                a = a.reshape(dec_batch, t_new, a.shape[-1])
                a = jnp.pad(a, ((0, 0), (n_pad, 0), (0, 0)))
                return a.reshape(dec_batch * tp, a.shape[-1])

            lane_is_ig = (jnp.arange(LANES) < HEADS_B)[None, None, :]
            pad_gates = jnp.broadcast_to(jnp.where(lane_is_ig, NEG, 0.0).astype(F32), (dec_batch, n_pad, LANES))
            g3 = jnp.concatenate([pad_gates, gates.reshape(dec_batch, t_new, LANES)], axis=1)
            gates_t_pad = jnp.swapaxes(g3[:, :, :n_gates], 1, 2)
            state = (state_mlstm_C[ib], state_mlstm_n[ib][:, :, None, :],
                     jnp.broadcast_to(state_mlstm_m[ib][:, :, None, None], (dec_batch, HEADS_B, 1, LANES)))
            gated, c_f, n_f, m_f = _mlstm_core(front_pad(qkv), front_pad(og), g3.reshape(dec_batch * tp, LANES),
                                               gates_t_pad, gh, ib, dec_batch, tp, tp, dqk, dv, state)
            gated = gated[:, n_pad:].reshape(ms, HEADS_B * dv)
            xs = _matmul_residual(gated, w_out, ib, xs, ms)
            c_s.append(c_f)
            n_s.append(n_f[:, :, 0, :])
            m_s.append(m_f[:, :, 0, 0])

        background = None
        if layer == depth - 2:
            background = _cache_shift_jobs(caches, [jnp.stack(new) for new in kv_new])
        elif layer == depth - 1:
            background = _kv_stack_jobs(kv_p, seq)
        xp, tails, *extra = _conv_ffn(xp, g_ffn, layer, w_up, ffn_conv_w, conv_b, w_down, tm_p,
                                      FFN_ROW_SUBTILE, seq=seq, background=background)
        if layer == depth - 2:
            kv_s = extra
        elif layer == depth - 1:
            kv_p = [a.reshape(a.shape[:2] + (-1, 2) + tile) for a in extra]
        tiles_per_seq = seq // tm_p
        tails = tails[tiles_per_seq - 1::tiles_per_seq, :, SUBLANES - (CONV_W - 1):, :]
        conv_p.append(jnp.swapaxes(tails, 1, 2).reshape(batch, CONV_W - 1, 2 * d_ff))
        st = state_ffn_conv[layer].reshape(dec_batch, CONV_W - 1, 2, d_ff)
        zeros = jnp.zeros((dec_batch, t_new, 2, d_ff), F32)

        def part_major(a):
            return jnp.transpose(a, (2, 0, 1, 3)).reshape(2, ms, d_ff)

        s1 = part_major(zeros.at[:, 0].set(st[:, 1]))
        s2 = part_major(zeros.at[:, 0].set(st[:, 0]).at[:, 1].set(st[:, 1]))
        xs, u_s = _conv_ffn(xs, g_ffn, layer, w_up, ffn_conv_w, conv_b, w_down, ms, ms,
                            sample_state=(s1, s2), t_new=t_new)
        u_s = jnp.moveaxis(u_s, 0, 1).reshape(dec_batch, t_new, 2 * d_ff)
        ext = jnp.concatenate([state_ffn_conv[layer], u_s], axis=1)
        conv_s.append(ext[:, t_new:])

    stack = jnp.stack
    return (xp.reshape(batch, seq, d), xs.reshape(dec_batch, t_new, d),
            kv_p[0], kv_s[0], kv_p[1], kv_s[1], kv_p[2], kv_s[2],
            stack(c_p), stack(c_s), stack(n_p), stack(n_s), stack(m_p), stack(m_s),
            stack(conv_p), stack(conv_s))
```

```python
import functools

import numpy as np
import jax
import jax.numpy as jnp
from jax import lax
from jax.experimental import pallas as pl
from jax.experimental.pallas import tpu as pltpu

F32 = jnp.float32
BF16 = jnp.bfloat16

DIL_PATTERNS = ((128, 1), (512, 4), (2048, 16))
N_GROUPS = len(DIL_PATTERNS)
HEADS_A = 8
HEAD_DIM = 128
ATT_W = HEADS_A * HEAD_DIM
BAND = 128
ROPE_THETA = 10000.0
PAST_LEN = 16384
HEADS_B = 8
GATE_CAP = 15.0
CONV_W = 3
EPS = 1e-6
NEG = -1e30

LANES = 128
SUBLANES = 8
VMEM_LIMIT_BYTES = 60 * 1024 * 1024
ROW_TILE = 1024
ATTN_ROW_TILE = 512
ATTN_STEP_ROWS = 1024
ROW_SUBTILE = 256
PERM_TILE = 256
PROJ_COL_TILE = 1024
FFN_COL_TILE = 512
FFN_ROW_SUBTILE = 512
MLSTM_CHUNK = 256
SAMPLE_PAD_T = 16


def _cparams(sem):
    return pltpu.CompilerParams(dimension_semantics=sem, vmem_limit_bytes=VMEM_LIMIT_BYTES)


def _rms_rows(x, g):
    ms = jnp.mean(x * x, axis=-1, keepdims=True)
    return (x * lax.rsqrt(ms + EPS)) * g


def _log_sigmoid(x):
    return jnp.minimum(x, 0.0) - jnp.log1p(jnp.exp(-jnp.abs(x)))


def _dot(a, b):
    return jnp.dot(a, b, preferred_element_type=F32)


def _dot_nt(a, b):
    return lax.dot_general(a, b, (((1,), (1,)), ((), ())), preferred_element_type=F32)


def _dot_tn(a, b):
    return lax.dot_general(a, b, (((0,), (0,)), ((), ())), preferred_element_type=F32)


def _sum(xs):
    return functools.reduce(lambda a, b: a + b, xs)


def _lane_fold(x, op):
    w = x.shape[-1]
    if w <= LANES or w % LANES:
        return x
    return functools.reduce(op, [x[:, i:i + LANES] for i in range(0, w, LANES)])


def _split3(x):
    hi = x.astype(BF16)
    r = x - hi.astype(F32)
    mid = r.astype(BF16)
    lo = (r - mid.astype(F32)).astype(BF16)
    return hi, mid, lo


def _select_rows(sel, x):
    return _sum([_dot(sel, p) for p in _split3(x)])


def _layer_spec(block, index_map, layer):
    return pl.BlockSpec((None,) + block, lambda *idx: (layer,) + index_map(*idx))


def _norm_kernel(x_ref, g_ref, h_ref):
    h_ref[...] = _rms_rows(x_ref[...], g_ref[...]).astype(BF16)


def _norm(x, gains, layer, tm):
    m, d = x.shape
    return pl.pallas_call(
        _norm_kernel,
        grid=(m // tm,),
        in_specs=[pl.BlockSpec((tm, d), lambda i: (i, 0)),
                  _layer_spec((1, d), lambda i: (0, 0), layer)],
        out_specs=pl.BlockSpec((tm, d), lambda i: (i, 0)),
        out_shape=jax.ShapeDtypeStruct((m, d), BF16),
        compiler_params=_cparams(("arbitrary",)),
        name="rms_norm",
    )(x, gains)


def _group_proj_kernel(*refs, rs, has_perm):
    h_ref, w_ref, gains_ref, cos_ref, sin_ref = refs[:5]
    rest = refs[5:]
    if has_perm:
        perm_ref, rest = rest[0], rest[1:]
    q_ref, k_ref, v_ref, kv_ref = rest
    tm = h_ref.shape[0]
    n_sub = tm // rs
    kv_rows = 2 * HEADS_A

    def head_cols(y, kind):
        return [y[:, kind * ATT_W + hh * HEAD_DIM:kind * ATT_W + (hh + 1) * HEAD_DIM] for hh in range(HEADS_A)]

    def rope(heads, gain, cos, sin):
        outs = []
        for y in heads:
            yn = _rms_rows(y, gain)
            outs.append(yn * cos + pltpu.roll(yn, HEAD_DIM // 2, 1) * sin)
        return outs

    def finish(r, y):
        rows = slice(r * rs, (r + 1) * rs)
        cos = cos_ref[rows, :]
        sin = sin_ref[rows, :]
        q = rope(head_cols(y, 0), gains_ref[0:1, :], cos, sin)
        k = rope(head_cols(y, 1), gains_ref[1:2, :], cos, sin)
        v = head_cols(y, 2)
        for kind, heads in enumerate((k, v)):
            for hh in range(HEADS_A):
                kv_ref[pl.ds(r * rs * kv_rows + kind * HEADS_A + hh, rs, stride=kv_rows), :] = heads[hh]
        for heads, ref in ((q, q_ref), (k, k_ref), (v, v_ref)):
            val = jnp.concatenate(heads, axis=1).astype(BF16)
            if has_perm:
                pt = perm_ref.shape[0]
                for c in range(rs // pt):
                    piece = _dot(perm_ref[...], val[c * pt:(c + 1) * pt, :]).astype(BF16)
                    ref[r * rs + c * pt:r * rs + (c + 1) * pt, :] = piece
            else:
                ref[rows, :] = val

    y = _dot(h_ref[0:rs, :], w_ref[...])
    for r in range(n_sub):
        y_next = _dot(h_ref[(r + 1) * rs:(r + 2) * rs, :], w_ref[...]) if r + 1 < n_sub else None
        finish(r, y)
        y = y_next


def _group_proj(h, w_all, layer, g, gains, cos, sin, perm, tm, rs):
    m, d = h.shape
    n_pos_tiles = cos.shape[0] // tm
    in_specs = [
        pl.BlockSpec((tm, d), lambda i: (i, 0)),
        _layer_spec((d, 3 * ATT_W), lambda i: (0, g), layer),
        pl.BlockSpec((2, HEAD_DIM), lambda i: (0, 0)),
        pl.BlockSpec((tm, HEAD_DIM), lambda i: (i % n_pos_tiles, 0)),
        pl.BlockSpec((tm, HEAD_DIM), lambda i: (i % n_pos_tiles, 0)),
    ]
    args = [h, w_all, gains, cos, sin]
    if perm is not None:
        in_specs.append(pl.BlockSpec(perm.shape, lambda i: (0, 0)))
        args.append(perm)
    row_spec = pl.BlockSpec((tm, ATT_W), lambda i: (i, 0))
    kv_rows = 2 * HEADS_A
    kernel = functools.partial(_group_proj_kernel, rs=rs, has_perm=perm is not None)
    return pl.pallas_call(
        kernel,
        grid=(m // tm,),
        in_specs=in_specs,
        out_specs=[row_spec, row_spec, row_spec, pl.BlockSpec((tm * kv_rows, HEAD_DIM), lambda i: (i, 0))],
        out_shape=[jax.ShapeDtypeStruct((m, ATT_W), BF16)] * 3
        + [jax.ShapeDtypeStruct((m * kv_rows, HEAD_DIM), F32)],
        compiler_params=_cparams(("arbitrary",)),
        name=f"attn_proj_g{g}",
    )(*args)


def _attn_prompt_kernel(*refs, tq, n_chunks, has_prev, n_res):
    q_ref, k_ref, v_ref = refs[:3]
    rest = refs[3:]
    if has_prev:
        kp_ref, vp_ref = rest[:2]
        rest = rest[2:]
    o_ref, lse_ref, qbufs, kbufs, vbufs, obufs, lbufs = rest
    n = pl.program_id(2)
    scale = HEAD_DIM ** -0.5
    cr = tq // n_chunks

    qi = lax.broadcasted_iota(jnp.int32, (BAND, 2 * BAND), 0)
    kj = lax.broadcasted_iota(jnp.int32, (BAND, 2 * BAND), 1)
    bias = jnp.where((kj >= qi) & (kj <= qi + BAND), 0.0, NEG).astype(F32)
    first_pen = jnp.where(n == 0, NEG, 0.0).astype(F32) if has_prev else NEG
    bias_first = bias + jnp.where(kj < BAND, first_pen, 0.0)
    lane = lax.broadcasted_iota(jnp.int32, (BAND, LANES), 1)

    for rr in range(n_res):
        qbuf, kbuf, vbuf, obuf, lbuf = (buf.at[rr] for buf in (qbufs, kbufs, vbufs, obufs, lbufs))
        if has_prev:
            kbuf[0:BAND, :] = kp_ref[0, 0, rr]
            vbuf[0:BAND, :] = vp_ref[0, 0, rr]
        else:
            kbuf[0:BAND, :] = jnp.zeros((BAND, ATT_W), BF16)
            vbuf[0:BAND, :] = jnp.zeros((BAND, ATT_W), BF16)
        for t in range(n_chunks):
            qbuf[t * cr:(t + 1) * cr, :] = q_ref[0, t, rr]
            kbuf[BAND + t * cr:BAND + (t + 1) * cr, :] = k_ref[0, t, rr]
            vbuf[BAND + t * cr:BAND + (t + 1) * cr, :] = v_ref[0, t, rr]

        for c in range(tq // BAND):
            rows = slice(c * BAND, (c + 1) * BAND)
            krows = slice(c * BAND, c * BAND + 2 * BAND)
            b = bias_first if c == 0 else bias
            heads = range(HEADS_A)
            cols = [slice(h * HEAD_DIM, (h + 1) * HEAD_DIM) for h in heads]
            s = [_dot_nt(qbuf[rows, cols[h]], kbuf[krows, cols[h]]) * scale + b for h in heads]
            mx = [jnp.max(_lane_fold(s[h], jnp.maximum), axis=-1, keepdims=True) for h in heads]
            p = [jnp.exp(s[h] - mx[h]) for h in heads]
            den = [jnp.sum(_lane_fold(p[h], jnp.add), axis=-1, keepdims=True) for h in heads]
            pv = [_dot(p[h].astype(BF16), vbuf[krows, cols[h]]) for h in heads]
            lse_tile = jnp.zeros((BAND, LANES), F32)
            for h in heads:
                obuf[rows, cols[h]] = pv[h] / den[h]
                lse_tile = jnp.where(lane == h, mx[h] + jnp.log(den[h]), lse_tile)
            lbuf[rows, :] = lse_tile

        for t in range(n_chunks):
            o_ref[0, t, rr] = obuf[t * cr:(t + 1) * cr, :]
            lse_ref[0, t, rr] = lbuf[t * cr:(t + 1) * cr, :]


def _attn_prompt(q, k, v, g, batch, seq):
    _, dil = DIL_PATTERNS[g]
    l = seq // dil
    if dil == 1:
        tq, n_chunks, n_res = min(l, 512), 1, 1
        view = (batch, l // tq, 1, tq)
        block = (1, 1, 1, tq)
        index = lambda b, r, n: (b, n, 0, 0, 0)
    else:
        tq, n_chunks, n_res = l, seq // PERM_TILE, max(1, ATTN_STEP_ROWS // l)
        view = (batch, n_chunks, dil, PERM_TILE // dil)
        block = (1, n_chunks, n_res, PERM_TILE // dil)
        index = lambda b, r, n: (b, 0, r, 0, 0)
    nq = l // tq
    has_prev = nq > 1

    def spec(width):
        return pl.BlockSpec(block + (width,), index)

    in_specs = [spec(ATT_W)] * 3
    args = [a.reshape(view + (ATT_W,)) for a in (q, k, v)]
    if has_prev:
        sub = tq // BAND
        prev_spec = pl.BlockSpec((1, 1, 1, BAND, ATT_W),
                                 lambda b, r, n: (b, jnp.maximum(n * sub - 1, 0), 0, 0, 0))
        in_specs += [prev_spec] * 2
        args += [a.reshape(batch, l // BAND, 1, BAND, ATT_W) for a in (k, v)]
    kernel = functools.partial(_attn_prompt_kernel, tq=tq, n_chunks=n_chunks, has_prev=has_prev, n_res=n_res)
    o, lse = pl.pallas_call(
        kernel,
        grid=(batch, dil // n_res, nq),
        in_specs=in_specs,
        out_specs=[spec(ATT_W), spec(LANES)],
        out_shape=[jax.ShapeDtypeStruct(view + (ATT_W,), F32), jax.ShapeDtypeStruct(view + (LANES,), F32)],
        scratch_shapes=[pltpu.VMEM((n_res, tq, ATT_W), BF16), pltpu.VMEM((n_res, BAND + tq, ATT_W), BF16),
                        pltpu.VMEM((n_res, BAND + tq, ATT_W), BF16), pltpu.VMEM((n_res, tq, ATT_W), F32),
                        pltpu.VMEM((n_res, tq, LANES), F32)],
        compiler_params=_cparams(("arbitrary", "arbitrary", "arbitrary")),
        name=f"attn_prompt_g{g}",
    )(*args)
    return o.reshape(batch * seq, ATT_W), lse.reshape(batch * seq, LANES)


def _combine_proj_kernel(o0_ref, l0_ref, o1_ref, l1_ref, o2_ref, l2_ref, s1_ref, s2_ref,
                         w_ref, x_ref, out_ref, *, rs):
    tm = x_ref.shape[0]
    for r in range(tm // rs):
        rows = slice(r * rs, (r + 1) * rs)
        os_ = [o0_ref[rows, :], _select_rows(s1_ref[...], o1_ref[rows, :]),
               _select_rows(s2_ref[...], o2_ref[rows, :])]
        ls = [l0_ref[rows, :], _select_rows(s1_ref[...], l1_ref[rows, :]),
              _select_rows(s2_ref[...], l2_ref[rows, :])]
        top = functools.reduce(jnp.maximum, ls)
        es = [jnp.exp(l - top) for l in ls]
        tot = _sum(es)
        alphas = [e / tot for e in es]
        heads = []
        for h in range(HEADS_A):
            cols = slice(h * HEAD_DIM, (h + 1) * HEAD_DIM)
            heads.append(_sum([a[:, h:h + 1] * o[:, cols] for a, o in zip(alphas, os_)]))
        comb = jnp.concatenate(heads, axis=1).astype(BF16)
        out_ref[rows, :] = x_ref[rows, :] + _dot(comb, w_ref[...])


def _combine_proj(os_, ls, sels, w_all, layer, x, tm, rs):
    m, d = x.shape
    o_spec = pl.BlockSpec((tm, ATT_W), lambda i: (i, 0))
    l_spec = pl.BlockSpec((tm, LANES), lambda i: (i, 0))
    sel_spec = pl.BlockSpec((rs, rs), lambda i: (0, 0))
    x_spec = pl.BlockSpec((tm, d), lambda i: (i, 0))
    kernel = functools.partial(_combine_proj_kernel, rs=rs)
    return pl.pallas_call(
        kernel,
        grid=(m // tm,),
        in_specs=[o_spec, l_spec, o_spec, l_spec, o_spec, l_spec, sel_spec, sel_spec,
                  _layer_spec((ATT_W, d), lambda i: (0, 0), layer), x_spec],
        out_specs=x_spec,
        out_shape=jax.ShapeDtypeStruct((m, d), F32),
        compiler_params=_cparams(("arbitrary",)),
        name="attn_combine_proj",
    )(os_[0], ls[0], os_[1], ls[1], os_[2], ls[2], sels[0], sels[1], w_all, x)


def _attn_sample_kernel(q_ref, kvn0_ref, kvn1_ref, kvn2_ref, c0_ref, c1_ref, c2_ref, o_ref, *, t_new):
    scale = HEAD_DIM ** -0.5
    caches = (c0_ref, c1_ref, c2_ref)
    news = (kvn0_ref, kvn1_ref, kvn2_ref)
    key_u = lax.broadcasted_iota(jnp.int32, (BAND, HEADS_A, 1), 0)

    for t in range(t_new):
        o_g, lse_g = [], []
        for g, (_, dil) in enumerate(DIL_PATTERNS):
            q = q_ref[g, 0, t]
            res = 0 if dil == 1 else t
            s_c = jnp.sum(caches[g][:, res, 0] * q[None], axis=-1, keepdims=True) * scale
            if dil == 1:
                s_c = jnp.where(key_u >= t, s_c, NEG)
                new_ts = list(range(t + 1))
            else:
                new_ts = [t]
            s_n = [jnp.sum(q * news[g][0, t2, 0], axis=-1, keepdims=True) * scale for t2 in new_ts]
            mx = functools.reduce(jnp.maximum, s_n, jnp.max(s_c, axis=0))
            p_c = jnp.exp(s_c - mx[None])
            den = jnp.sum(p_c, axis=0)
            acc = jnp.sum(p_c * caches[g][:, res, 1], axis=0)
            for t2, s in zip(new_ts, s_n):
                p_n = jnp.exp(s - mx)
                den = den + p_n
                acc = acc + p_n * news[g][0, t2, 1]
            o_g.append(acc / den)
            lse_g.append(mx + jnp.log(den))
        top = functools.reduce(jnp.maximum, lse_g)
        es = [jnp.exp(l - top) for l in lse_g]
        tot = _sum(es)
        o_ref[0, t] = _sum([(e / tot) * o for e, o in zip(es, o_g)])


def _attn_sample(q, kv_new, caches, layer, batch, t_new):
    tile = (HEADS_A, HEAD_DIM)
    in_specs = [pl.BlockSpec((N_GROUPS, 1, t_new) + tile, lambda b: (0, b, 0, 0, 0))]
    in_specs += [pl.BlockSpec((1, t_new, 2) + tile, lambda b: (b, 0, 0, 0, 0))] * N_GROUPS
    args = [q] + list(kv_new)
    for (win, dil), c in zip(DIL_PATTERNS, caches):
        n_res = min(dil, t_new)
        in_specs.append(pl.BlockSpec((None, None, BAND, n_res, 2) + tile,
                                     lambda b: (layer, b, 0, 0, 0, 0, 0)))
        args.append(c.reshape(c.shape[:2] + (BAND, dil, 2) + tile))
    kernel = functools.partial(_attn_sample_kernel, t_new=t_new)
    return pl.pallas_call(
        kernel,
        grid=(batch,),
        in_specs=in_specs,
        out_specs=pl.BlockSpec((1, t_new) + tile, lambda b: (b, 0, 0, 0)),
        out_shape=jax.ShapeDtypeStruct((batch, t_new) + tile, F32),
        compiler_params=_cparams(("arbitrary",)),
        name="attn_sample",
    )(*args)


def _matmul_residual_kernel(a_ref, w_ref, x_ref, o_ref):
    o_ref[...] = x_ref[...] + _dot(a_ref[...].astype(BF16), w_ref[...])


def _matmul_residual(a, w_all, layer, x, tm):
    m, k = a.shape
    n = w_all.shape[-1]
    tn = PROJ_COL_TILE
    return pl.pallas_call(
        _matmul_residual_kernel,
        grid=(m // tm, n // tn),
        in_specs=[
            pl.BlockSpec((tm, k), lambda i, j: (i, 0)),
            _layer_spec((k, tn), lambda i, j: (0, j), layer),
            pl.BlockSpec((tm, tn), lambda i, j: (i, j)),
        ],
        out_specs=pl.BlockSpec((tm, tn), lambda i, j: (i, j)),
        out_shape=jax.ShapeDtypeStruct((m, n), F32),
        compiler_params=_cparams(("arbitrary", "arbitrary")),
        name="matmul_residual",
    )(a, w_all, x)


def _gate_act(pre, is_input_gate):
    capped = GATE_CAP * jnp.tanh(pre / GATE_CAP)
    return jnp.where(is_input_gate, capped, _log_sigmoid(capped))


def _mlstm_in_kernel(x_ref, g_ref, w_ref, wg_ref, bg_ref, bgt_ref,
                     qkv_ref, og_ref, gates_ref, gatest_ref, h_scr, *, k_tiles, n_qkv_tiles, k_scale, n_gates):
    j = pl.program_id(1)

    @pl.when(j == 0)
    def _():
        h = _rms_rows(x_ref[...], g_ref[...]).astype(BF16)
        h_scr[...] = h
        pre = _dot_nt(h, wg_ref[...]) + bg_ref[...]
        lane = lax.broadcasted_iota(jnp.int32, pre.shape, 1)
        gates_ref[...] = jnp.where(lane < n_gates, _gate_act(pre, lane < n_gates // 2), 0.0)
        pre_t = _dot_nt(wg_ref[0:n_gates, :], h) + bgt_ref[...]
        row = lax.broadcasted_iota(jnp.int32, pre_t.shape, 0)
        gatest_ref[...] = _gate_act(pre_t, row < n_gates // 2)

    y = _dot_nt(h_scr[...], w_ref[...])

    @pl.when(j < n_qkv_tiles)
    def _():
        is_k = (j >= k_tiles[0]) & (j < k_tiles[1])
        qkv_ref[...] = (y * jnp.where(is_k, k_scale, 1.0).astype(F32)).astype(BF16)

    @pl.when(j >= n_qkv_tiles)
    def _():
        og_ref[...] = y


def _mlstm_in_proj(x, gains, layer, wt_all, ib, bg_all, bgt_all, tm, dqk, dv):
    m, d = x.shape
    tn = PROJ_COL_TILE
    qk_w, v_w = HEADS_B * dqk, HEADS_B * dv
    n_main = 2 * qk_w + 2 * v_w
    n_gates = wt_all.shape[1] - n_main
    n_qkv_tiles = (2 * qk_w + v_w) // tn
    nj = n_main // tn
    kernel = functools.partial(_mlstm_in_kernel, k_tiles=(qk_w // tn, 2 * qk_w // tn),
                               n_qkv_tiles=n_qkv_tiles, k_scale=dqk ** -0.5, n_gates=n_gates)
    return pl.pallas_call(
        kernel,
        grid=(m // tm, nj),
        in_specs=[
            pl.BlockSpec((tm, d), lambda i, j: (i, 0)),
            _layer_spec((1, d), lambda i, j: (0, 0), layer),
            _layer_spec((tn, d), lambda i, j: (j, 0), ib),
            _layer_spec((LANES, d), lambda i, j: (n_main // LANES, 0), ib),
            _layer_spec((1, LANES), lambda i, j: (0, 0), ib),
            _layer_spec((n_gates, 1), lambda i, j: (0, 0), ib),
        ],
        out_specs=[
            pl.BlockSpec((tm, tn), lambda i, j: (i, jnp.minimum(j, n_qkv_tiles - 1))),
            pl.BlockSpec((tm, tn), lambda i, j: (i, jnp.maximum(j - n_qkv_tiles, 0))),
            pl.BlockSpec((tm, LANES), lambda i, j: (i, 0)),
            pl.BlockSpec((n_gates, tm), lambda i, j: (0, i)),
        ],
        out_shape=[
            jax.ShapeDtypeStruct((m, 2 * qk_w + v_w), BF16),
            jax.ShapeDtypeStruct((m, v_w), F32),
            jax.ShapeDtypeStruct((m, LANES), F32),
            jax.ShapeDtypeStruct((n_gates, m), F32),
        ],
        scratch_shapes=[pltpu.VMEM((tm, d), BF16)],
        compiler_params=_cparams(("arbitrary", "arbitrary")),
        name="mlstm_in_proj",
    )(x, gains, wt_all, wt_all, bg_all, bgt_all)


def _mlstm_core_kernel(*refs, chunk, dqk, dv, has_state):
    q_ref, k_ref, v_ref, og_ref, gates_ref, gatest_ref, gh_ref = refs[:7]
    rest = refs[7:]
    if has_state:
        c0_ref, n0_ref, m0_ref = rest[:3]
        rest = rest[3:]
    out_ref, c_out, n_out, m_out, c_scr, n_scr, m_scr = rest
    c = pl.program_id(1)
    n_heads = c_scr.shape[0]

    @pl.when(c == 0)
    def _():
        if has_state:
            c_scr[...] = c0_ref[0]
            n_scr[...] = n0_ref[0]
            m_scr[...] = m0_ref[0]
        else:
            c_scr[...] = jnp.zeros_like(c_scr)
            n_scr[...] = jnp.zeros_like(n_scr)
            m_scr[...] = jnp.zeros_like(m_scr)

    row = lax.broadcasted_iota(jnp.int32, (chunk, chunk), 0)
    col = lax.broadcasted_iota(jnp.int32, (chunk, chunk), 1)
    causal = col <= row
    tri_lo = jnp.where(causal, 1.0, 0.0).astype(BF16)
    tri_up = jnp.where(row <= col, 1.0, 0.0).astype(BF16)

    gates = gates_ref[0]
    gates_t = gatest_ref[0] if len(gatest_ref.shape) == 3 else gatest_ref[...]
    b_cols = _sum([_dot(tri_lo, p) for p in _split3(gates)])
    b_rows = _sum([_dot(p, tri_up) for p in _split3(gates_t)])

    heads = range(n_heads)
    q = [q_ref[0, :, hh * dqk:(hh + 1) * dqk] for hh in heads]
    k = [k_ref[0, :, hh * dqk:(hh + 1) * dqk] for hh in heads]
    v = [v_ref[0, :, hh * dv:(hh + 1) * dv] for hh in heads]
    ig_col = [gates[:, hh:hh + 1] for hh in heads]
    b_col = [b_cols[:, n_heads + hh:n_heads + hh + 1] for hh in heads]
    m_prev = [m_scr[hh][:, 0:1] for hh in heads]

    inter, m_t, pexp = [], [], []
    for hh in heads:
        a_row = gates_t[hh:hh + 1, :] - b_rows[n_heads + hh:n_heads + hh + 1, :]
        a_mask = jnp.where(causal, a_row, NEG)
        a_max = jnp.max(_lane_fold(a_mask, jnp.maximum), axis=-1, keepdims=True)
        inter.append(b_col[hh] + m_prev[hh])
        m_t.append(jnp.maximum(inter[hh], b_col[hh] + a_max))
        pexp.append(jnp.exp(a_mask + (b_col[hh] - m_t[hh])))
    s = [_dot_nt(q[hh], k[hh]) for hh in heads]
    qc = [_dot_nt(q[hh], c_scr[hh].astype(BF16)) for hh in heads]
    p = [pexp[hh] * s[hh] for hh in heads]
    pv = [_dot(p[hh].astype(BF16), v[hh]) for hh in heads]
    for hh in heads:
        w_inter = jnp.exp(inter[hh] - m_t[hh])
        num = w_inter * qc[hh] + pv[hh]
        nq = jnp.sum(q[hh].astype(F32) * n_scr[hh], axis=-1, keepdims=True)
        den = w_inter * nq + jnp.sum(_lane_fold(p[hh], jnp.add), axis=-1, keepdims=True)
        r = 1.0 / jnp.maximum(jnp.abs(den), jnp.exp(-m_t[hh]))
        ms_num = jnp.sum(_lane_fold(num * num, jnp.add), axis=-1, keepdims=True) / dv
        scale = r * lax.rsqrt(r * r * ms_num + EPS)
        cols = slice(hh * dv, (hh + 1) * dv)
        gated = num * scale * gh_ref[:, cols] * jax.nn.sigmoid(og_ref[0, :, cols])
        out_ref[0, :, cols] = gated.astype(BF16)

    w_c, vw, w_s = [], [], []
    for hh in heads:
        b_last = b_col[hh][chunk - 1:chunk, :]
        dec = b_last - b_col[hh] + ig_col[hh]
        m_new = jnp.maximum(b_last + m_prev[hh], jnp.max(dec, axis=0, keepdims=True))
        w_c.append(jnp.exp(b_last + m_prev[hh] - m_new))
        w_s.append(jnp.exp(dec - m_new))
        vw.append((v[hh].astype(F32) * w_s[hh]).astype(BF16))
        m_scr[hh] = jnp.broadcast_to(m_new, (1, LANES))
    upd = [_dot_tn(vw[hh], k[hh]) for hh in heads]
    for hh in heads:
        c_scr[hh] = w_c[hh] * c_scr[hh] + upd[hh]
        n_scr[hh] = w_c[hh] * n_scr[hh] + jnp.sum(k[hh].astype(F32) * w_s[hh], axis=0, keepdims=True)

    @pl.when(c == pl.num_programs(1) - 1)
    def _():
        c_out[0] = c_scr[...]
        n_out[0] = n_scr[...]
        m_out[0] = m_scr[...]


def _mlstm_core(qkv, og, gates, gates_t, gh_all, ib, batch, seq, chunk, dqk, dv, state):
    n_heads = HEADS_B
    qk_w, v_w = n_heads * dqk, n_heads * dv
    nc = seq // chunk
    qkv3 = qkv.reshape(batch, seq, 2 * qk_w + v_w)
    state_specs = [
        pl.BlockSpec((1, n_heads, dv, dqk), lambda b, c: (b, 0, 0, 0)),
        pl.BlockSpec((1, n_heads, 1, dqk), lambda b, c: (b, 0, 0, 0)),
        pl.BlockSpec((1, n_heads, 1, LANES), lambda b, c: (b, 0, 0, 0)),
    ]
    in_specs = [
        pl.BlockSpec((1, chunk, qk_w), lambda b, c: (b, c, 0)),
        pl.BlockSpec((1, chunk, qk_w), lambda b, c: (b, c, 1)),
        pl.BlockSpec((1, chunk, v_w), lambda b, c: (b, c, 2 * qk_w // v_w)),
        pl.BlockSpec((1, chunk, v_w), lambda b, c: (b, c, 0)),
        pl.BlockSpec((1, chunk, LANES), lambda b, c: (b, c, 0)),
        pl.BlockSpec((2 * n_heads, chunk), lambda b, c: (0, b * nc + c)) if gates_t.ndim == 2
        else pl.BlockSpec((1, 2 * n_heads, chunk), lambda b, c: (b * nc + c, 0, 0)),
        _layer_spec((1, v_w), lambda b, c: (0, 0), ib),
    ]
    args = [qkv3, qkv3, qkv3, og.reshape(batch, seq, v_w), gates.reshape(batch, seq, LANES), gates_t, gh_all]
    if state is not None:
        in_specs += state_specs
        args += list(state)
    kernel = functools.partial(_mlstm_core_kernel, chunk=chunk, dqk=dqk, dv=dv,
                               has_state=state is not None)
    return pl.pallas_call(
        kernel,
        grid=(batch, nc),
        in_specs=in_specs,
        out_specs=[pl.BlockSpec((1, chunk, v_w), lambda b, c: (b, c, 0))] + state_specs,
        out_shape=[
            jax.ShapeDtypeStruct((batch, seq, v_w), BF16),
            jax.ShapeDtypeStruct((batch, n_heads, dv, dqk), F32),
            jax.ShapeDtypeStruct((batch, n_heads, 1, dqk), F32),
            jax.ShapeDtypeStruct((batch, n_heads, 1, LANES), F32),
        ],
        scratch_shapes=[pltpu.VMEM((n_heads, dv, dqk), F32), pltpu.VMEM((n_heads, 1, dqk), F32),
                        pltpu.VMEM((n_heads, 1, LANES), F32)],
        compiler_params=_cparams(("arbitrary", "arbitrary")),
        name="mlstm_core",
    )(*args)


def _ffn_kernel(*refs, rs, tiles_per_seq, t_new):
    x_ref, g_ref, wa_ref, wb_ref, cwa_ref, cwb_ref, cba_ref, cbb_ref, wd_ref = refs[:9]
    rest = refs[9:]
    sample = t_new is not None
    if sample:
        s1a_ref, s1b_ref, s2a_ref, s2b_ref, o_ref, ua_ref, ub_ref, h_scr = rest
        s1_refs, s2_refs, u_refs = (s1a_ref, s1b_ref), (s2a_ref, s2b_ref), (ua_ref, ub_ref)
    else:
        o_ref, tail_ref, h_scr, tail_scr = rest
    i = pl.program_id(0)
    f = pl.program_id(1)
    tm = x_ref.shape[0]
    n_sub = tm // rs

    @pl.when(f == 0)
    def _():
        x = x_ref[...]
        h_scr[...] = _rms_rows(x, g_ref[...]).astype(BF16)
        o_ref[...] = x

    parts = ((wa_ref, cwa_ref, cba_ref), (wb_ref, cwb_ref, cbb_ref))
    if not sample:
        @pl.when((i == 0) & (f == 0))
        def _():
            tail_scr[...] = jnp.zeros_like(tail_scr)

        seq_start = i % tiles_per_seq == 0
        prev = []
        for part in range(2):
            t = jnp.where(seq_start, 0.0, tail_scr[f, part])
            prev.append((t[SUBLANES - 2:SUBLANES - 1, :], t[SUBLANES - 1:SUBLANES, :]))

    def up(r):
        h = h_scr[r * rs:(r + 1) * rs, :]
        return [_dot(h, w_ref[...]) for w_ref, _, _ in parts]

    def conv_gate(r, us):
        conv = []
        for part, (_, cw_ref, cb_ref) in enumerate(parts):
            u = us[part]
            ridx = lax.broadcasted_iota(jnp.int32, u.shape, 0)
            r1 = pltpu.roll(u, 1, 0)
            r2 = pltpu.roll(u, 2, 0)
            if sample:
                u_refs[part][...] = u
                pos = ridx % t_new
                u1 = jnp.where(pos >= 1, r1, s1_refs[part][...])
                u2 = jnp.where(pos >= 2, r2, s2_refs[part][...])
            else:
                p2, p1 = prev[part]
                u1 = jnp.where(ridx == 0, p1, r1)
                u2 = jnp.where(ridx == 0, p2, jnp.where(ridx == 1, p1, r2))
                prev[part] = (u[rs - 2:rs - 1, :], u[rs - 1:rs, :])
                if r == n_sub - 1:
                    tail = u[rs - SUBLANES:rs, :]
                    tail_scr[f, part] = tail
                    tail_ref[0, part] = tail
            cw = cw_ref[...]
            conv.append(cb_ref[...] + cw[0:1, :] * u2 + cw[1:2, :] * u1 + cw[2:3, :] * u)
        return (jax.nn.silu(conv[0]) * conv[1]).astype(BF16)

    us = up(0)
    for r in range(n_sub):
        us_next = up(r + 1) if r + 1 < n_sub else None
        z = conv_gate(r, us)
        o_ref[r * rs:(r + 1) * rs, :] += _dot(z, wd_ref[...])
        us = us_next


def _conv_ffn(x, gains, layer, w_up_all, conv_w_all, conv_b_all, w_down_all, tm, rs,
              seq=None, sample_state=None, t_new=None):
    m, d = x.shape
    d_ff = w_down_all.shape[1]
    tf = FFN_COL_TILE
    nf = d_ff // tf
    sample = sample_state is not None
    in_specs = [
        pl.BlockSpec((tm, d), lambda i, f: (i, 0)),
        _layer_spec((1, d), lambda i, f: (0, 0), layer),
        _layer_spec((d, tf), lambda i, f: (0, f), layer),
        _layer_spec((d, tf), lambda i, f: (0, nf + f), layer),
        _layer_spec((CONV_W, tf), lambda i, f: (0, f), layer),
        _layer_spec((CONV_W, tf), lambda i, f: (0, nf + f), layer),
        _layer_spec((1, tf), lambda i, f: (0, f), layer),
        _layer_spec((1, tf), lambda i, f: (0, nf + f), layer),
        _layer_spec((tf, d), lambda i, f: (f, 0), layer),
    ]
    args = [x, gains, w_up_all, w_up_all, conv_w_all, conv_w_all, conv_b_all, conv_b_all, w_down_all]
    scratch = [pltpu.VMEM((tm, d), BF16)]
    if sample:
        tiles_per_seq = None
        half_specs = [pl.BlockSpec((tm, tf), lambda i, f: (i, f)), pl.BlockSpec((tm, tf), lambda i, f: (i, nf + f))]
        in_specs += half_specs * 2
        args += [sample_state[0]] * 2 + [sample_state[1]] * 2
        out_specs = [pl.BlockSpec((tm, d), lambda i, f: (i, 0))] + [pl.BlockSpec((tm, tf), lambda i, f: (i, f))] * 2
        out_shape = [jax.ShapeDtypeStruct((m, d), F32)] + [jax.ShapeDtypeStruct((m, d_ff), F32)] * 2
    else:
        tiles_per_seq = seq // tm
        out_specs = [pl.BlockSpec((tm, d), lambda i, f: (i, 0)),
                     pl.BlockSpec((1, 2, SUBLANES, tf), lambda i, f: (i, 0, 0, f))]
        out_shape = [jax.ShapeDtypeStruct((m, d), F32),
                     jax.ShapeDtypeStruct((m // tm, 2, SUBLANES, d_ff), F32)]
        scratch.append(pltpu.VMEM((nf, 2, SUBLANES, tf), F32))
    kernel = functools.partial(_ffn_kernel, rs=rs, tiles_per_seq=tiles_per_seq, t_new=t_new)
    return pl.pallas_call(
        kernel,
        grid=(m // tm, nf),
        in_specs=in_specs,
        out_specs=out_specs,
        out_shape=out_shape,
        scratch_shapes=scratch,
        compiler_params=_cparams(("arbitrary", "arbitrary")),
        name="conv_ffn_sample" if sample else "conv_ffn",
    )(*args)


def _shift_append(cache, new):
    n_buf, n_new = cache.shape[2], new.shape[2]
    zero = jnp.zeros((), cache.dtype)
    no_pad = (0, 0, 0)
    shifted = lax.pad(cache, zero, [no_pad, no_pad, (-n_new, n_new, 0)] + [no_pad] * (cache.ndim - 3))
    tail = lax.pad(new, zero, [no_pad, no_pad, (n_buf - n_new, 0, 0)] + [no_pad] * (cache.ndim - 3))
    row = lax.broadcasted_iota(jnp.int32, cache.shape, 2)
    return jnp.where(row < n_buf - n_new, shifted, tail)


def _rope_tables(pos):
    half = HEAD_DIM // 2
    inv_freq = ROPE_THETA ** (-jnp.arange(half, dtype=F32) / half)
    ang = pos.astype(F32)[:, None] * inv_freq[None, :]
    cos, sin = jnp.cos(ang), jnp.sin(ang)
    return jnp.concatenate([cos, cos], axis=-1), jnp.concatenate([-sin, sin], axis=-1)


def _perm_matrices(dil):
    pos = np.arange(PERM_TILE)
    slot = (pos % dil) * (PERM_TILE // dil) + pos // dil
    to_residue_major = np.zeros((PERM_TILE, PERM_TILE), np.float32)
    to_residue_major[slot, pos] = 1.0
    return jnp.asarray(to_residue_major, BF16), jnp.asarray(to_residue_major.T, BF16)


def kernel(x_prompt, x_sample, cache_kv_w128, cache_kv_w512, cache_kv_w2048, state_mlstm_C, state_mlstm_n, state_mlstm_m, state_ffn_conv, norm_mix, norm_ffn, attn_w_qkv, attn_q_norm, attn_k_norm, attn_w_o, mlstm_w_in, mlstm_b_gates, mlstm_norm_h, mlstm_w_out, ffn_w_up, ffn_conv_w, ffn_conv_b, ffn_w_down):
    batch, seq, d = x_prompt.shape
    dec_batch, t_new, _ = x_sample.shape
    depth = norm_mix.shape[0]
    caches = (cache_kv_w128, cache_kv_w512, cache_kv_w2048)
    dqk = state_mlstm_C.shape[-1]
    dv = state_mlstm_C.shape[-2]
    d_ff = ffn_w_down.shape[1]
    ms = dec_batch * t_new
    tm_p = ROW_TILE
    tile = (HEADS_A, HEAD_DIM)

    xp = x_prompt.reshape(batch * seq, d)
    xs = x_sample.reshape(ms, d)

    w_qkv, w_o, w_out, w_up, w_down = (a.astype(BF16) for a in (
        attn_w_qkv, attn_w_o, mlstm_w_out, ffn_w_up, ffn_w_down))
    g_mix = norm_mix[:, None, :]
    g_ffn = norm_ffn[:, None, :]
    conv_b = ffn_conv_b[:, None, :]
    n_main = 2 * HEADS_B * dqk + 2 * HEADS_B * dv
    n_gates = mlstm_b_gates.shape[-1]
    w_in_t = jnp.swapaxes(mlstm_w_in, 1, 2).astype(BF16)
    bg_row = jnp.zeros((mlstm_b_gates.shape[0], 1, LANES), F32).at[:, 0, :n_gates].set(mlstm_b_gates)
    bg_col = mlstm_b_gates[:, :, None]
    gh = mlstm_norm_h[:, None, :]

    cos_p, sin_p = _rope_tables(jnp.arange(seq))
    cos_s, sin_s = _rope_tables(PAST_LEN + jnp.arange(t_new))
    cos_s, sin_s = jnp.tile(cos_s, (dec_batch, 1)), jnp.tile(sin_s, (dec_batch, 1))
    perms = [None] + [_perm_matrices(dil) for _, dil in DIL_PATTERNS[1:]]

    kv_p = [[] for _ in DIL_PATTERNS]
    kv_new = [[] for _ in DIL_PATTERNS]
    c_p, n_p, m_p, c_s, n_s, m_s = [], [], [], [], [], []
    conv_p, conv_s = [], []

    for layer in range(depth):
        if layer % 2 == 0:
            ia = layer // 2
            gains = jnp.stack([attn_q_norm[ia], attn_k_norm[ia]])
            h = _norm(xp, g_mix, layer, tm_p)
            os_, ls = [], []
            for g, (win, dil) in enumerate(DIL_PATTERNS):
                perm = None if perms[g] is None else perms[g][0]
                q, k, v, kvf = _group_proj(h, w_qkv, ia, g, gains, cos_p, sin_p, perm,
                                           ATTN_ROW_TILE, ATTN_ROW_TILE)
                keep = min(win, seq)
                kv_p[g].append(kvf.reshape((batch, seq, 2) + tile)[:, seq - keep:])
                o_g, l_g = _attn_prompt(q, k, v, g, batch, seq)
                os_.append(o_g)
                ls.append(l_g)
            xp = _combine_proj(os_, ls, [perms[1][1], perms[2][1]], w_o, ia, xp, ATTN_ROW_TILE, ROW_SUBTILE)
            h = _norm(xs, g_mix, layer, ms)
            qs, news = [], []
            for g in range(N_GROUPS):
                q, _, _, kvf = _group_proj(h, w_qkv, ia, g, gains, cos_s, sin_s, None, ms, ms)
                qs.append(q.astype(F32).reshape((dec_batch, t_new) + tile))
                news.append(kvf.reshape((dec_batch, t_new, 2) + tile))
                kv_new[g].append(news[-1])
            comb_s = _attn_sample(jnp.stack(qs), news, caches, ia, dec_batch, t_new)
            xs = _matmul_residual(comb_s.reshape(ms, ATT_W), w_o, ia, xs, ms)
        else:
            ib = layer // 2
            qkv, og, gates, gates_t = _mlstm_in_proj(xp, g_mix, layer, w_in_t, ib, bg_row, bg_col,
                                                     tm_p, dqk, dv)
            gated, c_f, n_f, m_f = _mlstm_core(qkv, og, gates, gates_t, gh, ib, batch, seq,
                                               MLSTM_CHUNK, dqk, dv, None)
            xp = _matmul_residual(gated.reshape(batch * seq, HEADS_B * dv), w_out, ib, xp, tm_p)
            c_p.append(c_f)
            n_p.append(n_f[:, :, 0, :])
            m_p.append(m_f[:, :, 0, 0])
            qkv, og, gates, _ = _mlstm_in_proj(xs, g_mix, layer, w_in_t, ib, bg_row, bg_col, ms, dqk, dv)
            tp = SAMPLE_PAD_T
            n_pad = tp - t_new

            def front_pad(a):
                a = a.reshape(dec_batch, t_new, a.shape[-1])
                a = jnp.pad(a, ((0, 0), (n_pad, 0), (0, 0)))
                return a.reshape(dec_batch * tp, a.shape[-1])

            lane_is_ig = (jnp.arange(LANES) < HEADS_B)[None, None, :]
            pad_gates = jnp.broadcast_to(jnp.where(lane_is_ig, NEG, 0.0).astype(F32), (dec_batch, n_pad, LANES))
            g3 = jnp.concatenate([pad_gates, gates.reshape(dec_batch, t_new, LANES)], axis=1)
            gates_t_pad = jnp.swapaxes(g3[:, :, :n_gates], 1, 2)
            state = (state_mlstm_C[ib], state_mlstm_n[ib][:, :, None, :],
                     jnp.broadcast_to(state_mlstm_m[ib][:, :, None, None], (dec_batch, HEADS_B, 1, LANES)))
            gated, c_f, n_f, m_f = _mlstm_core(front_pad(qkv), front_pad(og), g3.reshape(dec_batch * tp, LANES),
                                               gates_t_pad, gh, ib, dec_batch, tp, tp, dqk, dv, state)
            gated = gated[:, n_pad:].reshape(ms, HEADS_B * dv)
            xs = _matmul_residual(gated, w_out, ib, xs, ms)
            c_s.append(c_f)
            n_s.append(n_f[:, :, 0, :])
            m_s.append(m_f[:, :, 0, 0])

        xp, tails = _conv_ffn(xp, g_ffn, layer, w_up, ffn_conv_w, conv_b, w_down, tm_p, FFN_ROW_SUBTILE, seq=seq)
        tiles_per_seq = seq // tm_p
        tails = tails[tiles_per_seq - 1::tiles_per_seq, :, SUBLANES - (CONV_W - 1):, :]
        conv_p.append(jnp.swapaxes(tails, 1, 2).reshape(batch, CONV_W - 1, 2 * d_ff))
        st = state_ffn_conv[layer]
        pad_rows = t_new - (CONV_W - 1)
        s2 = jnp.pad(st, ((0, 0), (0, pad_rows), (0, 0))).reshape(ms, 2 * d_ff)
        s1 = jnp.pad(st[:, 1:], ((0, 0), (0, pad_rows + 1), (0, 0))).reshape(ms, 2 * d_ff)
        xs, u_a, u_b = _conv_ffn(xs, g_ffn, layer, w_up, ffn_conv_w, conv_b, w_down, ms, ms,
                                 sample_state=(s1, s2), t_new=t_new)
        u_s = jnp.concatenate([u_a, u_b], axis=-1).reshape(dec_batch, t_new, 2 * d_ff)
        ext = jnp.concatenate([state_ffn_conv[layer], u_s], axis=1)
        conv_s.append(ext[:, t_new:])

    stack = jnp.stack
    kv_s = [_shift_append(c, stack(new)) for c, new in zip(caches, kv_new)]
    return (xp.reshape(batch, seq, d), xs.reshape(dec_batch, t_new, d),
            stack(kv_p[0]), kv_s[0], stack(kv_p[1]), kv_s[1], stack(kv_p[2]), kv_s[2],
            stack(c_p), stack(c_s), stack(n_p), stack(n_s), stack(m_p), stack(m_s),
            stack(conv_p), stack(conv_s))
```

```python
import functools

import numpy as np
import jax
import jax.numpy as jnp
from jax import lax
from jax.experimental import pallas as pl
from jax.experimental.pallas import tpu as pltpu

F32 = jnp.float32
BF16 = jnp.bfloat16

DIL_PATTERNS = ((128, 1), (512, 4), (2048, 16))
N_GROUPS = len(DIL_PATTERNS)
HEADS_A = 8
HEAD_DIM = 128
ATT_W = HEADS_A * HEAD_DIM
BAND = 128
ROPE_THETA = 10000.0
PAST_LEN = 16384
HEADS_B = 8
GATE_CAP = 15.0
CONV_W = 3
EPS = 1e-6
NEG = -1e30

LANES = 128
SUBLANES = 8
VMEM_LIMIT_BYTES = 60 * 1024 * 1024
ROW_TILE = 1024
ATTN_ROW_TILE = 512
ATTN_STEP_ROWS = 1024
ROW_SUBTILE = 256
PERM_TILE = 256
PROJ_COL_TILE = 1024
FFN_COL_TILE = 512
FFN_ROW_SUBTILE = 512
MLSTM_CHUNK = 512
SAMPLE_PAD_T = 16


def _cparams(sem):
    return pltpu.CompilerParams(dimension_semantics=sem, vmem_limit_bytes=VMEM_LIMIT_BYTES)


def _rms_rows(x, g):
    ms = jnp.mean(x * x, axis=-1, keepdims=True)
    return (x * lax.rsqrt(ms + EPS)) * g


def _log_sigmoid(x):
    return jnp.minimum(x, 0.0) - jnp.log1p(jnp.exp(-jnp.abs(x)))


def _dot(a, b):
    return jnp.dot(a, b, preferred_element_type=F32)


def _dot_nt(a, b):
    return lax.dot_general(a, b, (((1,), (1,)), ((), ())), preferred_element_type=F32)


def _dot_tn(a, b):
    return lax.dot_general(a, b, (((0,), (0,)), ((), ())), preferred_element_type=F32)


def _sum(xs):
    return functools.reduce(lambda a, b: a + b, xs)


def _lane_fold(x, op):
    w = x.shape[-1]
    if w <= LANES or w % LANES:
        return x
    return functools.reduce(op, [x[:, i:i + LANES] for i in range(0, w, LANES)])


def _split3(x):
    hi = x.astype(BF16)
    r = x - hi.astype(F32)
    mid = r.astype(BF16)
    lo = (r - mid.astype(F32)).astype(BF16)
    return hi, mid, lo


def _select_rows(sel, x):
    return _sum([_dot(sel, p) for p in _split3(x)])


def _layer_spec(block, index_map, layer):
    return pl.BlockSpec((None,) + block, lambda *idx: (layer,) + index_map(*idx))


def _norm_kernel(x_ref, g_ref, h_ref):
    h_ref[...] = _rms_rows(x_ref[...], g_ref[...]).astype(BF16)


def _norm(x, gains, layer, tm):
    m, d = x.shape
    return pl.pallas_call(
        _norm_kernel,
        grid=(m // tm,),
        in_specs=[pl.BlockSpec((tm, d), lambda i: (i, 0)),
                  _layer_spec((1, d), lambda i: (0, 0), layer)],
        out_specs=pl.BlockSpec((tm, d), lambda i: (i, 0)),
        out_shape=jax.ShapeDtypeStruct((m, d), BF16),
        compiler_params=_cparams(("arbitrary",)),
        name="rms_norm",
    )(x, gains)


def _group_proj_kernel(*refs, rs, has_perm):
    h_ref, w_ref, gains_ref, cos_ref, sin_ref = refs[:5]
    rest = refs[5:]
    if has_perm:
        perm_ref, rest = rest[0], rest[1:]
    q_ref, k_ref, v_ref, kv_ref = rest
    tm = h_ref.shape[0]
    n_sub = tm // rs
    kv_rows = 2 * HEADS_A

    def head_cols(y, kind):
        return [y[:, kind * ATT_W + hh * HEAD_DIM:kind * ATT_W + (hh + 1) * HEAD_DIM] for hh in range(HEADS_A)]

    def rope(heads, gain, cos, sin):
        outs = []
        for y in heads:
            yn = _rms_rows(y, gain)
            outs.append(yn * cos + pltpu.roll(yn, HEAD_DIM // 2, 1) * sin)
        return outs

    def finish(r, y):
        rows = slice(r * rs, (r + 1) * rs)
        cos = cos_ref[rows, :]
        sin = sin_ref[rows, :]
        q = rope(head_cols(y, 0), gains_ref[0:1, :], cos, sin)
        k = rope(head_cols(y, 1), gains_ref[1:2, :], cos, sin)
        v = head_cols(y, 2)
        for kind, heads in enumerate((k, v)):
            for hh in range(HEADS_A):
                kv_ref[pl.ds(r * rs * kv_rows + kind * HEADS_A + hh, rs, stride=kv_rows), :] = heads[hh]
        for heads, ref in ((q, q_ref), (k, k_ref), (v, v_ref)):
            val = jnp.concatenate(heads, axis=1).astype(BF16)
            if has_perm:
                pt = perm_ref.shape[0]
                for c in range(rs // pt):
                    piece = _dot(perm_ref[...], val[c * pt:(c + 1) * pt, :]).astype(BF16)
                    ref[r * rs + c * pt:r * rs + (c + 1) * pt, :] = piece
            else:
                ref[rows, :] = val

    y = _dot(h_ref[0:rs, :], w_ref[...])
    for r in range(n_sub):
        y_next = _dot(h_ref[(r + 1) * rs:(r + 2) * rs, :], w_ref[...]) if r + 1 < n_sub else None
        finish(r, y)
        y = y_next


def _group_proj(h, w_all, layer, g, gains, cos, sin, perm, tm, rs):
    m, d = h.shape
    n_pos_tiles = cos.shape[0] // tm
    in_specs = [
        pl.BlockSpec((tm, d), lambda i: (i, 0)),
        _layer_spec((d, 3 * ATT_W), lambda i: (0, g), layer),
        pl.BlockSpec((2, HEAD_DIM), lambda i: (0, 0)),
        pl.BlockSpec((tm, HEAD_DIM), lambda i: (i % n_pos_tiles, 0)),
        pl.BlockSpec((tm, HEAD_DIM), lambda i: (i % n_pos_tiles, 0)),
    ]
    args = [h, w_all, gains, cos, sin]
    if perm is not None:
        in_specs.append(pl.BlockSpec(perm.shape, lambda i: (0, 0)))
        args.append(perm)
    row_spec = pl.BlockSpec((tm, ATT_W), lambda i: (i, 0))
    kv_rows = 2 * HEADS_A
    kernel = functools.partial(_group_proj_kernel, rs=rs, has_perm=perm is not None)
    return pl.pallas_call(
        kernel,
        grid=(m // tm,),
        in_specs=in_specs,
        out_specs=[row_spec, row_spec, row_spec, pl.BlockSpec((tm * kv_rows, HEAD_DIM), lambda i: (i, 0))],
        out_shape=[jax.ShapeDtypeStruct((m, ATT_W), BF16)] * 3
        + [jax.ShapeDtypeStruct((m * kv_rows, HEAD_DIM), F32)],
        compiler_params=_cparams(("arbitrary",)),
        name=f"attn_proj_g{g}",
    )(*args)


def _attn_prompt_kernel(*refs, tq, n_chunks, has_prev, n_res):
    q_ref, k_ref, v_ref = refs[:3]
    rest = refs[3:]
    if has_prev:
        kp_ref, vp_ref = rest[:2]
        rest = rest[2:]
    o_ref, lse_ref, qbufs, kbufs, vbufs, obufs, lbufs = rest
    n = pl.program_id(2)
    scale = HEAD_DIM ** -0.5
    cr = tq // n_chunks

    qi = lax.broadcasted_iota(jnp.int32, (BAND, 2 * BAND), 0)
    kj = lax.broadcasted_iota(jnp.int32, (BAND, 2 * BAND), 1)
    bias = jnp.where((kj >= qi) & (kj <= qi + BAND), 0.0, NEG).astype(F32)
    first_pen = jnp.where(n == 0, NEG, 0.0).astype(F32) if has_prev else NEG
    bias_first = bias + jnp.where(kj < BAND, first_pen, 0.0)
    lane = lax.broadcasted_iota(jnp.int32, (BAND, LANES), 1)

    for rr in range(n_res):
        qbuf, kbuf, vbuf, obuf, lbuf = (buf.at[rr] for buf in (qbufs, kbufs, vbufs, obufs, lbufs))
        if has_prev:
            kbuf[0:BAND, :] = kp_ref[0, 0, rr]
            vbuf[0:BAND, :] = vp_ref[0, 0, rr]
        else:
            kbuf[0:BAND, :] = jnp.zeros((BAND, ATT_W), BF16)
            vbuf[0:BAND, :] = jnp.zeros((BAND, ATT_W), BF16)
        for t in range(n_chunks):
            qbuf[t * cr:(t + 1) * cr, :] = q_ref[0, t, rr]
            kbuf[BAND + t * cr:BAND + (t + 1) * cr, :] = k_ref[0, t, rr]
            vbuf[BAND + t * cr:BAND + (t + 1) * cr, :] = v_ref[0, t, rr]

        for c in range(tq // BAND):
            rows = slice(c * BAND, (c + 1) * BAND)
            krows = slice(c * BAND, c * BAND + 2 * BAND)
            b = bias_first if c == 0 else bias
            heads = range(HEADS_A)
            cols = [slice(h * HEAD_DIM, (h + 1) * HEAD_DIM) for h in heads]
            s = [_dot_nt(qbuf[rows, cols[h]], kbuf[krows, cols[h]]) * scale + b for h in heads]
            mx = [jnp.max(_lane_fold(s[h], jnp.maximum), axis=-1, keepdims=True) for h in heads]
            p = [jnp.exp(s[h] - mx[h]) for h in heads]
            den = [jnp.sum(_lane_fold(p[h], jnp.add), axis=-1, keepdims=True) for h in heads]
            pv = [_dot(p[h].astype(BF16), vbuf[krows, cols[h]]) for h in heads]
            lse_tile = jnp.zeros((BAND, LANES), F32)
            for h in heads:
                obuf[rows, cols[h]] = pv[h] / den[h]
                lse_tile = jnp.where(lane == h, mx[h] + jnp.log(den[h]), lse_tile)
            lbuf[rows, :] = lse_tile

        for t in range(n_chunks):
            o_ref[0, t, rr] = obuf[t * cr:(t + 1) * cr, :]
            lse_ref[0, t, rr] = lbuf[t * cr:(t + 1) * cr, :]


def _attn_prompt(q, k, v, g, batch, seq):
    _, dil = DIL_PATTERNS[g]
    l = seq // dil
    if dil == 1:
        tq, n_chunks, n_res = min(l, 512), 1, 1
        view = (batch, l // tq, 1, tq)
        block = (1, 1, 1, tq)
        index = lambda b, r, n: (b, n, 0, 0, 0)
    else:
        tq, n_chunks, n_res = l, seq // PERM_TILE, max(1, ATTN_STEP_ROWS // l)
        view = (batch, n_chunks, dil, PERM_TILE // dil)
        block = (1, n_chunks, n_res, PERM_TILE // dil)
        index = lambda b, r, n: (b, 0, r, 0, 0)
    nq = l // tq
    has_prev = nq > 1

    def spec(width):
        return pl.BlockSpec(block + (width,), index)

    in_specs = [spec(ATT_W)] * 3
    args = [a.reshape(view + (ATT_W,)) for a in (q, k, v)]
    if has_prev:
        sub = tq // BAND
        prev_spec = pl.BlockSpec((1, 1, 1, BAND, ATT_W),
                                 lambda b, r, n: (b, jnp.maximum(n * sub - 1, 0), 0, 0, 0))
        in_specs += [prev_spec] * 2
        args += [a.reshape(batch, l // BAND, 1, BAND, ATT_W) for a in (k, v)]
    kernel = functools.partial(_attn_prompt_kernel, tq=tq, n_chunks=n_chunks, has_prev=has_prev, n_res=n_res)
    o, lse = pl.pallas_call(
        kernel,
        grid=(batch, dil // n_res, nq),
        in_specs=in_specs,
        out_specs=[spec(ATT_W), spec(LANES)],
        out_shape=[jax.ShapeDtypeStruct(view + (ATT_W,), F32), jax.ShapeDtypeStruct(view + (LANES,), F32)],
        scratch_shapes=[pltpu.VMEM((n_res, tq, ATT_W), BF16), pltpu.VMEM((n_res, BAND + tq, ATT_W), BF16),
                        pltpu.VMEM((n_res, BAND + tq, ATT_W), BF16), pltpu.VMEM((n_res, tq, ATT_W), F32),
                        pltpu.VMEM((n_res, tq, LANES), F32)],
        compiler_params=_cparams(("arbitrary", "arbitrary", "arbitrary")),
        name=f"attn_prompt_g{g}",
    )(*args)
    return o.reshape(batch * seq, ATT_W), lse.reshape(batch * seq, LANES)


def _combine_proj_kernel(o0_ref, l0_ref, o1_ref, l1_ref, o2_ref, l2_ref, s1_ref, s2_ref,
                         w_ref, x_ref, out_ref, *, rs):
    tm = x_ref.shape[0]
    for r in range(tm // rs):
        rows = slice(r * rs, (r + 1) * rs)
        os_ = [o0_ref[rows, :], _select_rows(s1_ref[...], o1_ref[rows, :]),
               _select_rows(s2_ref[...], o2_ref[rows, :])]
        ls = [l0_ref[rows, :], _select_rows(s1_ref[...], l1_ref[rows, :]),
              _select_rows(s2_ref[...], l2_ref[rows, :])]
        top = functools.reduce(jnp.maximum, ls)
        es = [jnp.exp(l - top) for l in ls]
        tot = _sum(es)
        alphas = [e / tot for e in es]
        heads = []
        for h in range(HEADS_A):
            cols = slice(h * HEAD_DIM, (h + 1) * HEAD_DIM)
            heads.append(_sum([a[:, h:h + 1] * o[:, cols] for a, o in zip(alphas, os_)]))
        comb = jnp.concatenate(heads, axis=1).astype(BF16)
        out_ref[rows, :] = x_ref[rows, :] + _dot(comb, w_ref[...])


def _combine_proj(os_, ls, sels, w_all, layer, x, tm, rs):
    m, d = x.shape
    o_spec = pl.BlockSpec((tm, ATT_W), lambda i: (i, 0))
    l_spec = pl.BlockSpec((tm, LANES), lambda i: (i, 0))
    sel_spec = pl.BlockSpec((rs, rs), lambda i: (0, 0))
    x_spec = pl.BlockSpec((tm, d), lambda i: (i, 0))
    kernel = functools.partial(_combine_proj_kernel, rs=rs)
    return pl.pallas_call(
        kernel,
        grid=(m // tm,),
        in_specs=[o_spec, l_spec, o_spec, l_spec, o_spec, l_spec, sel_spec, sel_spec,
                  _layer_spec((ATT_W, d), lambda i: (0, 0), layer), x_spec],
        out_specs=x_spec,
        out_shape=jax.ShapeDtypeStruct((m, d), F32),
        compiler_params=_cparams(("arbitrary",)),
        name="attn_combine_proj",
    )(os_[0], ls[0], os_[1], ls[1], os_[2], ls[2], sels[0], sels[1], w_all, x)


def _attn_sample_kernel(q_ref, kvn0_ref, kvn1_ref, kvn2_ref, c0_ref, c1_ref, c2_ref, o_ref, *, t_new):
    scale = HEAD_DIM ** -0.5
    caches = (c0_ref, c1_ref, c2_ref)
    news = (kvn0_ref, kvn1_ref, kvn2_ref)
    key_u = lax.broadcasted_iota(jnp.int32, (BAND, HEADS_A, 1), 0)

    for t in range(t_new):
        o_g, lse_g = [], []
        for g, (_, dil) in enumerate(DIL_PATTERNS):
            q = q_ref[g, 0, t]
            res = 0 if dil == 1 else t
            s_c = jnp.sum(caches[g][:, res, 0] * q[None], axis=-1, keepdims=True) * scale
            if dil == 1:
                s_c = jnp.where(key_u >= t, s_c, NEG)
                new_ts = list(range(t + 1))
            else:
                new_ts = [t]
            s_n = [jnp.sum(q * news[g][0, t2, 0], axis=-1, keepdims=True) * scale for t2 in new_ts]
            mx = functools.reduce(jnp.maximum, s_n, jnp.max(s_c, axis=0))
            p_c = jnp.exp(s_c - mx[None])
            den = jnp.sum(p_c, axis=0)
            acc = jnp.sum(p_c * caches[g][:, res, 1], axis=0)
            for t2, s in zip(new_ts, s_n):
                p_n = jnp.exp(s - mx)
                den = den + p_n
                acc = acc + p_n * news[g][0, t2, 1]
            o_g.append(acc / den)
            lse_g.append(mx + jnp.log(den))
        top = functools.reduce(jnp.maximum, lse_g)
        es = [jnp.exp(l - top) for l in lse_g]
        tot = _sum(es)
        o_ref[0, t] = _sum([(e / tot) * o for e, o in zip(es, o_g)])


def _attn_sample(q, kv_new, caches, layer, batch, t_new):
    tile = (HEADS_A, HEAD_DIM)
    in_specs = [pl.BlockSpec((N_GROUPS, 1, t_new) + tile, lambda b: (0, b, 0, 0, 0))]
    in_specs += [pl.BlockSpec((1, t_new, 2) + tile, lambda b: (b, 0, 0, 0, 0))] * N_GROUPS
    args = [q] + list(kv_new)
    for (win, dil), c in zip(DIL_PATTERNS, caches):
        n_res = min(dil, t_new)
        in_specs.append(pl.BlockSpec((None, None, BAND, n_res, 2) + tile,
                                     lambda b: (layer, b, 0, 0, 0, 0, 0)))
        args.append(c.reshape(c.shape[:2] + (BAND, dil, 2) + tile))
    kernel = functools.partial(_attn_sample_kernel, t_new=t_new)
    return pl.pallas_call(
        kernel,
        grid=(batch,),
        in_specs=in_specs,
        out_specs=pl.BlockSpec((1, t_new) + tile, lambda b: (b, 0, 0, 0)),
        out_shape=jax.ShapeDtypeStruct((batch, t_new) + tile, F32),
        compiler_params=_cparams(("arbitrary",)),
        name="attn_sample",
    )(*args)


def _matmul_residual_kernel(a_ref, w_ref, x_ref, o_ref):
    o_ref[...] = x_ref[...] + _dot(a_ref[...].astype(BF16), w_ref[...])


def _matmul_residual(a, w_all, layer, x, tm):
    m, k = a.shape
    n = w_all.shape[-1]
    tn = PROJ_COL_TILE
    return pl.pallas_call(
        _matmul_residual_kernel,
        grid=(m // tm, n // tn),
        in_specs=[
            pl.BlockSpec((tm, k), lambda i, j: (i, 0)),
            _layer_spec((k, tn), lambda i, j: (0, j), layer),
            pl.BlockSpec((tm, tn), lambda i, j: (i, j)),
        ],
        out_specs=pl.BlockSpec((tm, tn), lambda i, j: (i, j)),
        out_shape=jax.ShapeDtypeStruct((m, n), F32),
        compiler_params=_cparams(("arbitrary", "arbitrary")),
        name="matmul_residual",
    )(a, w_all, x)


def _gate_act(pre, is_input_gate):
    capped = GATE_CAP * jnp.tanh(pre / GATE_CAP)
    return jnp.where(is_input_gate, capped, _log_sigmoid(capped))


def _mlstm_in_kernel(x_ref, g_ref, w_ref, wg_ref, bg_ref, bgt_ref,
                     qkv_ref, og_ref, gates_ref, gatest_ref, h_scr, *, k_tiles, n_qkv_tiles, k_scale, n_gates):
    j = pl.program_id(1)

    @pl.when(j == 0)
    def _():
        h = _rms_rows(x_ref[...], g_ref[...]).astype(BF16)
        h_scr[...] = h
        pre = _dot_nt(h, wg_ref[...]) + bg_ref[...]
        lane = lax.broadcasted_iota(jnp.int32, pre.shape, 1)
        gates_ref[...] = jnp.where(lane < n_gates, _gate_act(pre, lane < n_gates // 2), 0.0)
        pre_t = _dot_nt(wg_ref[0:n_gates, :], h) + bgt_ref[...]
        row = lax.broadcasted_iota(jnp.int32, pre_t.shape, 0)
        gatest_ref[...] = _gate_act(pre_t, row < n_gates // 2)

    y = _dot_nt(h_scr[...], w_ref[...])

    @pl.when(j < n_qkv_tiles)
    def _():
        is_k = (j >= k_tiles[0]) & (j < k_tiles[1])
        qkv_ref[...] = (y * jnp.where(is_k, k_scale, 1.0).astype(F32)).astype(BF16)

    @pl.when(j >= n_qkv_tiles)
    def _():
        og_ref[...] = y


def _mlstm_in_proj(x, gains, layer, wt_all, ib, bg_all, bgt_all, tm, dqk, dv):
    m, d = x.shape
    tn = PROJ_COL_TILE
    qk_w, v_w = HEADS_B * dqk, HEADS_B * dv
    n_main = 2 * qk_w + 2 * v_w
    n_gates = wt_all.shape[1] - n_main
    n_qkv_tiles = (2 * qk_w + v_w) // tn
    nj = n_main // tn
    kernel = functools.partial(_mlstm_in_kernel, k_tiles=(qk_w // tn, 2 * qk_w // tn),
                               n_qkv_tiles=n_qkv_tiles, k_scale=dqk ** -0.5, n_gates=n_gates)
    return pl.pallas_call(
        kernel,
        grid=(m // tm, nj),
        in_specs=[
            pl.BlockSpec((tm, d), lambda i, j: (i, 0)),
            _layer_spec((1, d), lambda i, j: (0, 0), layer),
            _layer_spec((tn, d), lambda i, j: (j, 0), ib),
            _layer_spec((LANES, d), lambda i, j: (n_main // LANES, 0), ib),
            _layer_spec((1, LANES), lambda i, j: (0, 0), ib),
            _layer_spec((n_gates, 1), lambda i, j: (0, 0), ib),
        ],
        out_specs=[
            pl.BlockSpec((tm, tn), lambda i, j: (i, jnp.minimum(j, n_qkv_tiles - 1))),
            pl.BlockSpec((tm, tn), lambda i, j: (i, jnp.maximum(j - n_qkv_tiles, 0))),
            pl.BlockSpec((tm, LANES), lambda i, j: (i, 0)),
            pl.BlockSpec((n_gates, tm), lambda i, j: (0, i)),
        ],
        out_shape=[
            jax.ShapeDtypeStruct((m, 2 * qk_w + v_w), BF16),
            jax.ShapeDtypeStruct((m, v_w), F32),
            jax.ShapeDtypeStruct((m, LANES), F32),
            jax.ShapeDtypeStruct((n_gates, m), F32),
        ],
        scratch_shapes=[pltpu.VMEM((tm, d), BF16)],
        compiler_params=_cparams(("arbitrary", "arbitrary")),
        name="mlstm_in_proj",
    )(x, gains, wt_all, wt_all, bg_all, bgt_all)


def _mlstm_core_kernel(*refs, chunk, dqk, dv, has_state):
    q_ref, k_ref, v_ref, og_ref, gates_ref, gatest_ref, gh_ref = refs[:7]
    rest = refs[7:]
    if has_state:
        c0_ref, n0_ref, m0_ref = rest[:3]
        rest = rest[3:]
    out_ref, c_out, n_out, m_out, c_scr, n_scr, m_scr = rest
    c = pl.program_id(1)
    n_heads = c_scr.shape[0]

    @pl.when(c == 0)
    def _():
        if has_state:
            c_scr[...] = c0_ref[0]
            n_scr[...] = n0_ref[0]
            m_scr[...] = m0_ref[0]
        else:
            c_scr[...] = jnp.zeros_like(c_scr)
            n_scr[...] = jnp.zeros_like(n_scr)
            m_scr[...] = jnp.zeros_like(m_scr)

    row = lax.broadcasted_iota(jnp.int32, (chunk, chunk), 0)
    col = lax.broadcasted_iota(jnp.int32, (chunk, chunk), 1)
    causal = col <= row
    tri_lo = jnp.where(causal, 1.0, 0.0).astype(BF16)
    tri_up = jnp.where(row <= col, 1.0, 0.0).astype(BF16)

    gates = gates_ref[0]
    gates_t = gatest_ref[0] if len(gatest_ref.shape) == 3 else gatest_ref[...]
    b_cols = _sum([_dot(tri_lo, p) for p in _split3(gates)])
    b_rows = _sum([_dot(p, tri_up) for p in _split3(gates_t)])

    heads = range(n_heads)
    q = [q_ref[0, :, hh * dqk:(hh + 1) * dqk] for hh in heads]
    k = [k_ref[0, :, hh * dqk:(hh + 1) * dqk] for hh in heads]
    v = [v_ref[0, :, hh * dv:(hh + 1) * dv] for hh in heads]
    ig_col = [gates[:, hh:hh + 1] for hh in heads]
    b_col = [b_cols[:, n_heads + hh:n_heads + hh + 1] for hh in heads]
    m_prev = [m_scr[hh][:, 0:1] for hh in heads]

    inter, m_t, pexp = [], [], []
    for hh in heads:
        a_row = gates_t[hh:hh + 1, :] - b_rows[n_heads + hh:n_heads + hh + 1, :]
        a_mask = jnp.where(causal, a_row, NEG)
        a_max = jnp.max(_lane_fold(a_mask, jnp.maximum), axis=-1, keepdims=True)
        inter.append(b_col[hh] + m_prev[hh])
        m_t.append(jnp.maximum(inter[hh], b_col[hh] + a_max))
        pexp.append(jnp.exp(a_mask + (b_col[hh] - m_t[hh])))
    s = [_dot_nt(q[hh], k[hh]) for hh in heads]
    qc = [_dot_nt(q[hh], c_scr[hh].astype(BF16)) for hh in heads]
    p = [pexp[hh] * s[hh] for hh in heads]
    pv = [_dot(p[hh].astype(BF16), v[hh]) for hh in heads]
    for hh in heads:
        w_inter = jnp.exp(inter[hh] - m_t[hh])
        num = w_inter * qc[hh] + pv[hh]
        nq = jnp.sum(q[hh].astype(F32) * n_scr[hh], axis=-1, keepdims=True)
        den = w_inter * nq + jnp.sum(_lane_fold(p[hh], jnp.add), axis=-1, keepdims=True)
        r = 1.0 / jnp.maximum(jnp.abs(den), jnp.exp(-m_t[hh]))
        ms_num = jnp.sum(_lane_fold(num * num, jnp.add), axis=-1, keepdims=True) / dv
        scale = r * lax.rsqrt(r * r * ms_num + EPS)
        cols = slice(hh * dv, (hh + 1) * dv)
        gated = num * scale * gh_ref[:, cols] * jax.nn.sigmoid(og_ref[0, :, cols])
        out_ref[0, :, cols] = gated.astype(BF16)

    w_c, vw, w_s = [], [], []
    for hh in heads:
        b_last = b_col[hh][chunk - 1:chunk, :]
        dec = b_last - b_col[hh] + ig_col[hh]
        m_new = jnp.maximum(b_last + m_prev[hh], jnp.max(dec, axis=0, keepdims=True))
        w_c.append(jnp.exp(b_last + m_prev[hh] - m_new))
        w_s.append(jnp.exp(dec - m_new))
        vw.append((v[hh].astype(F32) * w_s[hh]).astype(BF16))
        m_scr[hh] = jnp.broadcast_to(m_new, (1, LANES))
    upd = [_dot_tn(vw[hh], k[hh]) for hh in heads]
    for hh in heads:
        c_scr[hh] = w_c[hh] * c_scr[hh] + upd[hh]
        n_scr[hh] = w_c[hh] * n_scr[hh] + jnp.sum(k[hh].astype(F32) * w_s[hh], axis=0, keepdims=True)

    @pl.when(c == pl.num_programs(1) - 1)
    def _():
        c_out[0] = c_scr[...]
        n_out[0] = n_scr[...]
        m_out[0] = m_scr[...]


def _mlstm_core(qkv, og, gates, gates_t, gh_all, ib, batch, seq, chunk, dqk, dv, state):
    n_heads = HEADS_B
    qk_w, v_w = n_heads * dqk, n_heads * dv
    nc = seq // chunk
    qkv3 = qkv.reshape(batch, seq, 2 * qk_w + v_w)
    state_specs = [
        pl.BlockSpec((1, n_heads, dv, dqk), lambda b, c: (b, 0, 0, 0)),
        pl.BlockSpec((1, n_heads, 1, dqk), lambda b, c: (b, 0, 0, 0)),
        pl.BlockSpec((1, n_heads, 1, LANES), lambda b, c: (b, 0, 0, 0)),
    ]
    in_specs = [
        pl.BlockSpec((1, chunk, qk_w), lambda b, c: (b, c, 0)),
        pl.BlockSpec((1, chunk, qk_w), lambda b, c: (b, c, 1)),
        pl.BlockSpec((1, chunk, v_w), lambda b, c: (b, c, 2 * qk_w // v_w)),
        pl.BlockSpec((1, chunk, v_w), lambda b, c: (b, c, 0)),
        pl.BlockSpec((1, chunk, LANES), lambda b, c: (b, c, 0)),
        pl.BlockSpec((2 * n_heads, chunk), lambda b, c: (0, b * nc + c)) if gates_t.ndim == 2
        else pl.BlockSpec((1, 2 * n_heads, chunk), lambda b, c: (b * nc + c, 0, 0)),
        _layer_spec((1, v_w), lambda b, c: (0, 0), ib),
    ]
    args = [qkv3, qkv3, qkv3, og.reshape(batch, seq, v_w), gates.reshape(batch, seq, LANES), gates_t, gh_all]
    if state is not None:
        in_specs += state_specs
        args += list(state)
    kernel = functools.partial(_mlstm_core_kernel, chunk=chunk, dqk=dqk, dv=dv,
                               has_state=state is not None)
    return pl.pallas_call(
        kernel,
        grid=(batch, nc),
        in_specs=in_specs,
        out_specs=[pl.BlockSpec((1, chunk, v_w), lambda b, c: (b, c, 0))] + state_specs,
        out_shape=[
            jax.ShapeDtypeStruct((batch, seq, v_w), BF16),
            jax.ShapeDtypeStruct((batch, n_heads, dv, dqk), F32),
            jax.ShapeDtypeStruct((batch, n_heads, 1, dqk), F32),
            jax.ShapeDtypeStruct((batch, n_heads, 1, LANES), F32),
        ],
        scratch_shapes=[pltpu.VMEM((n_heads, dv, dqk), F32), pltpu.VMEM((n_heads, 1, dqk), F32),
                        pltpu.VMEM((n_heads, 1, LANES), F32)],
        compiler_params=_cparams(("arbitrary", "arbitrary")),
        name="mlstm_core",
    )(*args)


def _ffn_kernel(*refs, rs, tiles_per_seq, t_new, cast_next):
    x_ref, g_ref, wa_ref, wb_ref, cwa_ref, cwb_ref, cba_ref, cbb_ref, wd_ref = refs[:9]
    rest = refs[9:]
    sample = t_new is not None
    if sample:
        s1a_ref, s1b_ref, s2a_ref, s2b_ref, o_ref, ua_ref, ub_ref, h_scr = rest
        s1_refs, s2_refs, u_refs = (s1a_ref, s1b_ref), (s2a_ref, s2b_ref), (ua_ref, ub_ref)
    else:
        if cast_next:
            up_next_ref, down_next_ref, o_ref, tail_ref, up_cast_ref, down_cast_ref, h_scr, tail_scr = rest
            up_cast_ref[...] = up_next_ref[...].astype(BF16)
            down_cast_ref[...] = down_next_ref[...].astype(BF16)
        else:
            o_ref, tail_ref, h_scr, tail_scr = rest
    i = pl.program_id(0)
    f = pl.program_id(1)
    tm = x_ref.shape[0]
    n_sub = tm // rs

    @pl.when(f == 0)
    def _():
        x = x_ref[...]
        h_scr[...] = _rms_rows(x, g_ref[...]).astype(BF16)
        o_ref[...] = x

    parts = ((wa_ref, cwa_ref, cba_ref), (wb_ref, cwb_ref, cbb_ref))
    if not sample:
        @pl.when((i == 0) & (f == 0))
        def _():
            tail_scr[...] = jnp.zeros_like(tail_scr)

        seq_start = i % tiles_per_seq == 0
        prev = []
        for part in range(2):
            t = jnp.where(seq_start, 0.0, tail_scr[f, part])
            prev.append((t[SUBLANES - 2:SUBLANES - 1, :], t[SUBLANES - 1:SUBLANES, :]))

    def up(r):
        h = h_scr[r * rs:(r + 1) * rs, :]
        return [_dot(h, w_ref[...]) for w_ref, _, _ in parts]

    def conv_gate(r, us):
        conv = []
        for part, (_, cw_ref, cb_ref) in enumerate(parts):
            u = us[part]
            ridx = lax.broadcasted_iota(jnp.int32, u.shape, 0)
            r1 = pltpu.roll(u, 1, 0)
            r2 = pltpu.roll(u, 2, 0)
            if sample:
                u_refs[part][...] = u
                pos = ridx % t_new
                u1 = jnp.where(pos >= 1, r1, s1_refs[part][...])
                u2 = jnp.where(pos >= 2, r2, s2_refs[part][...])
            else:
                p2, p1 = prev[part]
                u1 = jnp.where(ridx == 0, p1, r1)
                u2 = jnp.where(ridx == 0, p2, jnp.where(ridx == 1, p1, r2))
                prev[part] = (u[rs - 2:rs - 1, :], u[rs - 1:rs, :])
                if r == n_sub - 1:
                    tail = u[rs - SUBLANES:rs, :]
                    tail_scr[f, part] = tail
                    tail_ref[0, part] = tail
            cw = cw_ref[...]
            conv.append(cb_ref[...] + cw[0:1, :] * u2 + cw[1:2, :] * u1 + cw[2:3, :] * u)
        return (jax.nn.silu(conv[0]) * conv[1]).astype(BF16)

    us = up(0)
    for r in range(n_sub):
        us_next = up(r + 1) if r + 1 < n_sub else None
        z = conv_gate(r, us)
        o_ref[r * rs:(r + 1) * rs, :] += _dot(z, wd_ref[...])
        us = us_next


def _conv_ffn(x, gains, layer, w_up_all, conv_w_all, conv_b_all, w_down_all, w_layer, tm, rs,
              seq=None, sample_state=None, t_new=None, cast_next=None):
    m, d = x.shape
    d_ff = w_down_all.shape[1]
    tf = FFN_COL_TILE
    nf = d_ff // tf
    sample = sample_state is not None
    in_specs = [
        pl.BlockSpec((tm, d), lambda i, f: (i, 0)),
        _layer_spec((1, d), lambda i, f: (0, 0), layer),
        _layer_spec((d, tf), lambda i, f: (0, f), w_layer),
        _layer_spec((d, tf), lambda i, f: (0, nf + f), w_layer),
        _layer_spec((CONV_W, tf), lambda i, f: (0, f), layer),
        _layer_spec((CONV_W, tf), lambda i, f: (0, nf + f), layer),
        _layer_spec((1, tf), lambda i, f: (0, f), layer),
        _layer_spec((1, tf), lambda i, f: (0, nf + f), layer),
        _layer_spec((tf, d), lambda i, f: (f, 0), w_layer),
    ]
    args = [x, gains, w_up_all, w_up_all, conv_w_all, conv_w_all, conv_b_all, conv_b_all, w_down_all]
    scratch = [pltpu.VMEM((tm, d), BF16)]
    if sample:
        tiles_per_seq = None
        half_specs = [pl.BlockSpec((tm, tf), lambda i, f: (i, f)), pl.BlockSpec((tm, tf), lambda i, f: (i, nf + f))]
        in_specs += half_specs * 2
        args += [sample_state[0]] * 2 + [sample_state[1]] * 2
        out_specs = [pl.BlockSpec((tm, d), lambda i, f: (i, 0))] + [pl.BlockSpec((tm, tf), lambda i, f: (i, f))] * 2
        out_shape = [jax.ShapeDtypeStruct((m, d), F32)] + [jax.ShapeDtypeStruct((m, d_ff), F32)] * 2
    else:
        tiles_per_seq = seq // tm
        out_specs = [pl.BlockSpec((tm, d), lambda i, f: (i, 0)),
                     pl.BlockSpec((1, 2, SUBLANES, tf), lambda i, f: (i, 0, 0, f))]
        out_shape = [jax.ShapeDtypeStruct((m, d), F32),
                     jax.ShapeDtypeStruct((m // tm, 2, SUBLANES, d_ff), F32)]
        scratch.append(pltpu.VMEM((nf, 2, SUBLANES, tf), F32))
        if cast_next is not None:
            up_f32, down_f32, next_layer = cast_next
            n_steps = (m // tm) * nf
            cw, rw = 2 * d_ff // n_steps, d_ff // n_steps
            assert cw % LANES == 0 and rw % SUBLANES == 0 and cw * n_steps == 2 * d_ff and rw * n_steps == d_ff
            in_specs += [_layer_spec((d, cw), lambda i, f: (0, i * nf + f), next_layer),
                         _layer_spec((rw, d), lambda i, f: (i * nf + f, 0), next_layer)]
            args += [up_f32, down_f32]
            out_specs += [_layer_spec((d, cw), lambda i, f: (0, i * nf + f), 0),
                          _layer_spec((rw, d), lambda i, f: (i * nf + f, 0), 0)]
            out_shape += [jax.ShapeDtypeStruct((1, d, 2 * d_ff), BF16), jax.ShapeDtypeStruct((1, d_ff, d), BF16)]
    kernel = functools.partial(_ffn_kernel, rs=rs, tiles_per_seq=tiles_per_seq, t_new=t_new,
                               cast_next=cast_next is not None)
    return pl.pallas_call(
        kernel,
        grid=(m // tm, nf),
        in_specs=in_specs,
        out_specs=out_specs,
        out_shape=out_shape,
        scratch_shapes=scratch,
        compiler_params=_cparams(("arbitrary", "arbitrary")),
        name="conv_ffn_sample" if sample else "conv_ffn",
    )(*args)


def _shift_append(cache, new):
    n_buf, n_new = cache.shape[2], new.shape[2]
    zero = jnp.zeros((), cache.dtype)
    no_pad = (0, 0, 0)
    shifted = lax.pad(cache, zero, [no_pad, no_pad, (-n_new, n_new, 0)] + [no_pad] * (cache.ndim - 3))
    tail = lax.pad(new, zero, [no_pad, no_pad, (n_buf - n_new, 0, 0)] + [no_pad] * (cache.ndim - 3))
    row = lax.broadcasted_iota(jnp.int32, cache.shape, 2)
    return jnp.where(row < n_buf - n_new, shifted, tail)


def _rope_tables(pos):
    half = HEAD_DIM // 2
    inv_freq = ROPE_THETA ** (-jnp.arange(half, dtype=F32) / half)
    ang = pos.astype(F32)[:, None] * inv_freq[None, :]
    cos, sin = jnp.cos(ang), jnp.sin(ang)
    return jnp.concatenate([cos, cos], axis=-1), jnp.concatenate([-sin, sin], axis=-1)


def _perm_matrices(dil):
    pos = np.arange(PERM_TILE)
    slot = (pos % dil) * (PERM_TILE // dil) + pos // dil
    to_residue_major = np.zeros((PERM_TILE, PERM_TILE), np.float32)
    to_residue_major[slot, pos] = 1.0
    return jnp.asarray(to_residue_major, BF16), jnp.asarray(to_residue_major.T, BF16)


def kernel(x_prompt, x_sample, cache_kv_w128, cache_kv_w512, cache_kv_w2048, state_mlstm_C, state_mlstm_n, state_mlstm_m, state_ffn_conv, norm_mix, norm_ffn, attn_w_qkv, attn_q_norm, attn_k_norm, attn_w_o, mlstm_w_in, mlstm_b_gates, mlstm_norm_h, mlstm_w_out, ffn_w_up, ffn_conv_w, ffn_conv_b, ffn_w_down):
    batch, seq, d = x_prompt.shape
    dec_batch, t_new, _ = x_sample.shape
    depth = norm_mix.shape[0]
    caches = (cache_kv_w128, cache_kv_w512, cache_kv_w2048)
    dqk = state_mlstm_C.shape[-1]
    dv = state_mlstm_C.shape[-2]
    d_ff = ffn_w_down.shape[1]
    ms = dec_batch * t_new
    tm_p = ROW_TILE
    tile = (HEADS_A, HEAD_DIM)

    xp = x_prompt.reshape(batch * seq, d)
    xs = x_sample.reshape(ms, d)

    w_qkv, w_o, w_out = (a.astype(BF16) for a in (attn_w_qkv, attn_w_o, mlstm_w_out))
    w_up, w_down = ffn_w_up[:1].astype(BF16), ffn_w_down[:1].astype(BF16)
    g_mix = norm_mix[:, None, :]
    g_ffn = norm_ffn[:, None, :]
    conv_b = ffn_conv_b[:, None, :]
    n_main = 2 * HEADS_B * dqk + 2 * HEADS_B * dv
    n_gates = mlstm_b_gates.shape[-1]
    w_in_t = jnp.swapaxes(mlstm_w_in, 1, 2).astype(BF16)
    bg_row = jnp.zeros((mlstm_b_gates.shape[0], 1, LANES), F32).at[:, 0, :n_gates].set(mlstm_b_gates)
    bg_col = mlstm_b_gates[:, :, None]
    gh = mlstm_norm_h[:, None, :]

    cos_p, sin_p = _rope_tables(jnp.arange(seq))
    cos_s, sin_s = _rope_tables(PAST_LEN + jnp.arange(t_new))
    cos_s, sin_s = jnp.tile(cos_s, (dec_batch, 1)), jnp.tile(sin_s, (dec_batch, 1))
    perms = [None] + [_perm_matrices(dil) for _, dil in DIL_PATTERNS[1:]]

    kv_p = [[] for _ in DIL_PATTERNS]
    kv_new = [[] for _ in DIL_PATTERNS]
    c_p, n_p, m_p, c_s, n_s, m_s = [], [], [], [], [], []
    conv_p, conv_s = [], []

    for layer in range(depth):
        if layer % 2 == 0:
            ia = layer // 2
            gains = jnp.stack([attn_q_norm[ia], attn_k_norm[ia]])
            h = _norm(xp, g_mix, layer, tm_p)
            os_, ls = [], []
            for g, (win, dil) in enumerate(DIL_PATTERNS):
                perm = None if perms[g] is None else perms[g][0]
                q, k, v, kvf = _group_proj(h, w_qkv, ia, g, gains, cos_p, sin_p, perm,
                                           ATTN_ROW_TILE, ATTN_ROW_TILE)
                keep = min(win, seq)
                kv_p[g].append(kvf.reshape((batch, seq, 2) + tile)[:, seq - keep:])
                o_g, l_g = _attn_prompt(q, k, v, g, batch, seq)
                os_.append(o_g)
                ls.append(l_g)
            xp = _combine_proj(os_, ls, [perms[1][1], perms[2][1]], w_o, ia, xp, ATTN_ROW_TILE, ROW_SUBTILE)
            h = _norm(xs, g_mix, layer, ms)
            qs, news = [], []
            for g in range(N_GROUPS):
                q, _, _, kvf = _group_proj(h, w_qkv, ia, g, gains, cos_s, sin_s, None, ms, ms)
                qs.append(q.astype(F32).reshape((dec_batch, t_new) + tile))
                news.append(kvf.reshape((dec_batch, t_new, 2) + tile))
                kv_new[g].append(news[-1])
            comb_s = _attn_sample(jnp.stack(qs), news, caches, ia, dec_batch, t_new)
            xs = _matmul_residual(comb_s.reshape(ms, ATT_W), w_o, ia, xs, ms)
        else:
            ib = layer // 2
            qkv, og, gates, gates_t = _mlstm_in_proj(xp, g_mix, layer, w_in_t, ib, bg_row, bg_col,
                                                     tm_p, dqk, dv)
            gated, c_f, n_f, m_f = _mlstm_core(qkv, og, gates, gates_t, gh, ib, batch, seq,
                                               MLSTM_CHUNK, dqk, dv, None)
            xp = _matmul_residual(gated.reshape(batch * seq, HEADS_B * dv), w_out, ib, xp, tm_p)
            c_p.append(c_f)
            n_p.append(n_f[:, :, 0, :])
            m_p.append(m_f[:, :, 0, 0])
            qkv, og, gates, _ = _mlstm_in_proj(xs, g_mix, layer, w_in_t, ib, bg_row, bg_col, ms, dqk, dv)
            tp = SAMPLE_PAD_T
            n_pad = tp - t_new

            def front_pad(a):
                a = a.reshape(dec_batch, t_new, a.shape[-1])
                a = jnp.pad(a, ((0, 0), (n_pad, 0), (0, 0)))
                return a.reshape(dec_batch * tp, a.shape[-1])

            lane_is_ig = (jnp.arange(LANES) < HEADS_B)[None, None, :]
            pad_gates = jnp.broadcast_to(jnp.where(lane_is_ig, NEG, 0.0).astype(F32), (dec_batch, n_pad, LANES))
            g3 = jnp.concatenate([pad_gates, gates.reshape(dec_batch, t_new, LANES)], axis=1)
            gates_t_pad = jnp.swapaxes(g3[:, :, :n_gates], 1, 2)
            state = (state_mlstm_C[ib], state_mlstm_n[ib][:, :, None, :],
                     jnp.broadcast_to(state_mlstm_m[ib][:, :, None, None], (dec_batch, HEADS_B, 1, LANES)))
            gated, c_f, n_f, m_f = _mlstm_core(front_pad(qkv), front_pad(og), g3.reshape(dec_batch * tp, LANES),
                                               gates_t_pad, gh, ib, dec_batch, tp, tp, dqk, dv, state)
            gated = gated[:, n_pad:].reshape(ms, HEADS_B * dv)
            xs = _matmul_residual(gated, w_out, ib, xs, ms)
            c_s.append(c_f)
            n_s.append(n_f[:, :, 0, :])
            m_s.append(m_f[:, :, 0, 0])

        cast_next = (ffn_w_up, ffn_w_down, layer + 1) if layer + 1 < depth else None
        xp, tails, *w_next = _conv_ffn(xp, g_ffn, layer, w_up, ffn_conv_w, conv_b, w_down, 0, tm_p,
                                       FFN_ROW_SUBTILE, seq=seq, cast_next=cast_next)
        tiles_per_seq = seq // tm_p
        tails = tails[tiles_per_seq - 1::tiles_per_seq, :, SUBLANES - (CONV_W - 1):, :]
        conv_p.append(jnp.swapaxes(tails, 1, 2).reshape(batch, CONV_W - 1, 2 * d_ff))
        st = state_ffn_conv[layer]
        pad_rows = t_new - (CONV_W - 1)
        s2 = jnp.pad(st, ((0, 0), (0, pad_rows), (0, 0))).reshape(ms, 2 * d_ff)
        s1 = jnp.pad(st[:, 1:], ((0, 0), (0, pad_rows + 1), (0, 0))).reshape(ms, 2 * d_ff)
        xs, u_a, u_b = _conv_ffn(xs, g_ffn, layer, w_up, ffn_conv_w, conv_b, w_down, 0, ms, ms,
                                 sample_state=(s1, s2), t_new=t_new)
        if w_next:
            w_up, w_down = w_next
        u_s = jnp.concatenate([u_a, u_b], axis=-1).reshape(dec_batch, t_new, 2 * d_ff)
        ext = jnp.concatenate([state_ffn_conv[layer], u_s], axis=1)
        conv_s.append(ext[:, t_new:])

    stack = jnp.stack
    kv_s = [_shift_append(c, stack(new)) for c, new in zip(caches, kv_new)]
    return (xp.reshape(batch, seq, d), xs.reshape(dec_batch, t_new, d),
            stack(kv_p[0]), kv_s[0], stack(kv_p[1]), kv_s[1], stack(kv_p[2]), kv_s[2],
            stack(c_p), stack(c_s), stack(n_p), stack(n_s), stack(m_p), stack(m_s),
            stack(conv_p), stack(conv_s))
```

```python
import functools

import numpy as np
import jax
import jax.numpy as jnp
from jax import lax
from jax.experimental import pallas as pl
from jax.experimental.pallas import tpu as pltpu

F32 = jnp.float32
BF16 = jnp.bfloat16

DIL_PATTERNS = ((128, 1), (512, 4), (2048, 16))
N_GROUPS = len(DIL_PATTERNS)
HEADS_A = 8
HEAD_DIM = 128
ATT_W = HEADS_A * HEAD_DIM
BAND = 128
ROPE_THETA = 10000.0
PAST_LEN = 16384
HEADS_B = 8
GATE_CAP = 15.0
CONV_W = 3
EPS = 1e-6
NEG = -1e30

LANES = 128
SUBLANES = 8
VMEM_LIMIT_BYTES = 60 * 1024 * 1024
ROW_TILE = 1024
ATTN_ROW_TILE = 512
ATTN_STEP_ROWS = 1024
ROW_SUBTILE = 256
PERM_TILE = 256
PROJ_COL_TILE = 1024
FFN_COL_TILE = 512
FFN_ROW_SUBTILE = 512
MLSTM_CHUNK = 512
SAMPLE_PAD_T = 16


def _cparams(sem):
    return pltpu.CompilerParams(dimension_semantics=sem, vmem_limit_bytes=VMEM_LIMIT_BYTES)


def _rms_rows(x, g):
    ms = jnp.mean(x * x, axis=-1, keepdims=True)
    return (x * lax.rsqrt(ms + EPS)) * g


def _log_sigmoid(x):
    return jnp.minimum(x, 0.0) - jnp.log1p(jnp.exp(-jnp.abs(x)))


def _dot(a, b):
    return jnp.dot(a, b, preferred_element_type=F32)


def _dot_nt(a, b):
    return lax.dot_general(a, b, (((1,), (1,)), ((), ())), preferred_element_type=F32)


def _dot_tn(a, b):
    return lax.dot_general(a, b, (((0,), (0,)), ((), ())), preferred_element_type=F32)


def _sum(xs):
    return functools.reduce(lambda a, b: a + b, xs)


def _lane_fold(x, op):
    w = x.shape[-1]
    if w <= LANES or w % LANES:
        return x
    return functools.reduce(op, [x[:, i:i + LANES] for i in range(0, w, LANES)])


def _split3(x):
    hi = x.astype(BF16)
    r = x - hi.astype(F32)
    mid = r.astype(BF16)
    lo = (r - mid.astype(F32)).astype(BF16)
    return hi, mid, lo


def _select_rows(sel, x):
    return _sum([_dot(sel, p) for p in _split3(x)])


def _layer_spec(block, index_map, layer):
    return pl.BlockSpec((None,) + block, lambda *idx: (layer,) + index_map(*idx))


def _norm_kernel(x_ref, g_ref, h_ref):
    h_ref[...] = _rms_rows(x_ref[...], g_ref[...]).astype(BF16)


def _norm(x, gains, layer, tm):
    m, d = x.shape
    return pl.pallas_call(
        _norm_kernel,
        grid=(m // tm,),
        in_specs=[pl.BlockSpec((tm, d), lambda i: (i, 0)),
                  _layer_spec((1, d), lambda i: (0, 0), layer)],
        out_specs=pl.BlockSpec((tm, d), lambda i: (i, 0)),
        out_shape=jax.ShapeDtypeStruct((m, d), BF16),
        compiler_params=_cparams(("arbitrary",)),
        name="rms_norm",
    )(x, gains)


def _group_proj_kernel(*refs, rs, has_perm, has_alias):
    h_ref, w_ref, gains_ref, cos_ref, sin_ref = refs[:5]
    rest = refs[5:]
    if has_perm:
        perm_ref, rest = rest[0], rest[1:]
    if has_alias:
        rest = rest[1:]
    q_ref, k_ref, v_ref, kv_ref = rest
    kv_slabs = [kv_ref.at[l] for l in range(kv_ref.shape[0])] if len(kv_ref.shape) == 3 else [kv_ref]
    tm = h_ref.shape[0]
    n_sub = tm // rs
    kv_rows = 2 * HEADS_A

    def head_cols(y, kind):
        return [y[:, kind * ATT_W + hh * HEAD_DIM:kind * ATT_W + (hh + 1) * HEAD_DIM] for hh in range(HEADS_A)]

    def rope(heads, gain, cos, sin):
        outs = []
        for y in heads:
            yn = _rms_rows(y, gain)
            outs.append(yn * cos + pltpu.roll(yn, HEAD_DIM // 2, 1) * sin)
        return outs

    def finish(r, y):
        rows = slice(r * rs, (r + 1) * rs)
        cos = cos_ref[rows, :]
        sin = sin_ref[rows, :]
        q = rope(head_cols(y, 0), gains_ref[0:1, :], cos, sin)
        k = rope(head_cols(y, 1), gains_ref[1:2, :], cos, sin)
        v = head_cols(y, 2)
        for kind, heads in enumerate((k, v)):
            for hh in range(HEADS_A):
                for slab in kv_slabs:
                    slab[pl.ds(r * rs * kv_rows + kind * HEADS_A + hh, rs, stride=kv_rows), :] = heads[hh]
        for heads, ref in ((q, q_ref), (k, k_ref), (v, v_ref)):
            val = jnp.concatenate(heads, axis=1).astype(BF16)
            if has_perm:
                pt = perm_ref.shape[0]
                for c in range(rs // pt):
                    piece = _dot(perm_ref[...], val[c * pt:(c + 1) * pt, :]).astype(BF16)
                    ref[r * rs + c * pt:r * rs + (c + 1) * pt, :] = piece
            else:
                ref[rows, :] = val

    y = _dot(h_ref[0:rs, :], w_ref[...])
    for r in range(n_sub):
        y_next = _dot(h_ref[(r + 1) * rs:(r + 2) * rs, :], w_ref[...]) if r + 1 < n_sub else None
        finish(r, y)
        y = y_next


def _group_proj(h, w_all, layer, g, gains, cos, sin, perm, tm, rs, kv_layers=None, kv_stack=None):
    m, d = h.shape
    n_pos_tiles = cos.shape[0] // tm
    in_specs = [
        pl.BlockSpec((tm, d), lambda i: (i, 0)),
        _layer_spec((d, 3 * ATT_W), lambda i: (0, g), layer),
        pl.BlockSpec((2, HEAD_DIM), lambda i: (0, 0)),
        pl.BlockSpec((tm, HEAD_DIM), lambda i: (i % n_pos_tiles, 0)),
        pl.BlockSpec((tm, HEAD_DIM), lambda i: (i % n_pos_tiles, 0)),
    ]
    args = [h, w_all, gains, cos, sin]
    if perm is not None:
        in_specs.append(pl.BlockSpec(perm.shape, lambda i: (0, 0)))
        args.append(perm)
    row_spec = pl.BlockSpec((tm, ATT_W), lambda i: (i, 0))
    kv_rows = 2 * HEADS_A
    aliases = {}
    if kv_layers is None:
        kv_spec = pl.BlockSpec((tm * kv_rows, HEAD_DIM), lambda i: (i, 0))
        kv_shape = jax.ShapeDtypeStruct((m * kv_rows, HEAD_DIM), F32)
    else:
        kv_shape = jax.ShapeDtypeStruct((kv_layers, m * kv_rows, HEAD_DIM), F32)
        if kv_stack is None:
            kv_spec = pl.BlockSpec((kv_layers, tm * kv_rows, HEAD_DIM), lambda i: (0, i, 0))
        else:
            kv_spec = _layer_spec((tm * kv_rows, HEAD_DIM), lambda i: (i, 0), layer)
            aliases = {len(args): 3}
            in_specs.append(pl.BlockSpec(memory_space=pl.ANY))
            args.append(kv_stack)
    kernel = functools.partial(_group_proj_kernel, rs=rs, has_perm=perm is not None, has_alias=bool(aliases))
    return pl.pallas_call(
        kernel,
        grid=(m // tm,),
        in_specs=in_specs,
        out_specs=[row_spec, row_spec, row_spec, kv_spec],
        out_shape=[jax.ShapeDtypeStruct((m, ATT_W), BF16)] * 3 + [kv_shape],
        input_output_aliases=aliases,
        compiler_params=_cparams(("arbitrary",)),
        name=f"attn_proj_g{g}",
    )(*args)


def _attn_prompt_kernel(*refs, tq, n_chunks, has_prev, n_res):
    q_ref, k_ref, v_ref = refs[:3]
    rest = refs[3:]
    if has_prev:
        kp_ref, vp_ref = rest[:2]
        rest = rest[2:]
    o_ref, lse_ref, qbufs, kbufs, vbufs, obufs, lbufs = rest
    n = pl.program_id(2)
    scale = HEAD_DIM ** -0.5
    cr = tq // n_chunks

    qi = lax.broadcasted_iota(jnp.int32, (BAND, 2 * BAND), 0)
    kj = lax.broadcasted_iota(jnp.int32, (BAND, 2 * BAND), 1)
    bias = jnp.where((kj >= qi) & (kj <= qi + BAND), 0.0, NEG).astype(F32)
    first_pen = jnp.where(n == 0, NEG, 0.0).astype(F32) if has_prev else NEG
    bias_first = bias + jnp.where(kj < BAND, first_pen, 0.0)
    lane = lax.broadcasted_iota(jnp.int32, (BAND, LANES), 1)

    for rr in range(n_res):
        qbuf, kbuf, vbuf, obuf, lbuf = (buf.at[rr] for buf in (qbufs, kbufs, vbufs, obufs, lbufs))
        if has_prev:
            kbuf[0:BAND, :] = kp_ref[0, 0, rr]
            vbuf[0:BAND, :] = vp_ref[0, 0, rr]
        else:
            kbuf[0:BAND, :] = jnp.zeros((BAND, ATT_W), BF16)
            vbuf[0:BAND, :] = jnp.zeros((BAND, ATT_W), BF16)
        for t in range(n_chunks):
            qbuf[t * cr:(t + 1) * cr, :] = q_ref[0, t, rr]
            kbuf[BAND + t * cr:BAND + (t + 1) * cr, :] = k_ref[0, t, rr]
            vbuf[BAND + t * cr:BAND + (t + 1) * cr, :] = v_ref[0, t, rr]

        for c in range(tq // BAND):
            rows = slice(c * BAND, (c + 1) * BAND)
            krows = slice(c * BAND, c * BAND + 2 * BAND)
            b = bias_first if c == 0 else bias
            heads = range(HEADS_A)
            cols = [slice(h * HEAD_DIM, (h + 1) * HEAD_DIM) for h in heads]
            s = [_dot_nt(qbuf[rows, cols[h]], kbuf[krows, cols[h]]) * scale + b for h in heads]
            mx = [jnp.max(_lane_fold(s[h], jnp.maximum), axis=-1, keepdims=True) for h in heads]
            p = [jnp.exp(s[h] - mx[h]) for h in heads]
            den = [jnp.sum(_lane_fold(p[h], jnp.add), axis=-1, keepdims=True) for h in heads]
            pv = [_dot(p[h].astype(BF16), vbuf[krows, cols[h]]) for h in heads]
            lse_tile = jnp.zeros((BAND, LANES), F32)
            for h in heads:
                obuf[rows, cols[h]] = pv[h] / den[h]
                lse_tile = jnp.where(lane == h, mx[h] + jnp.log(den[h]), lse_tile)
            lbuf[rows, :] = lse_tile

        for t in range(n_chunks):
            o_ref[0, t, rr] = obuf[t * cr:(t + 1) * cr, :]
            lse_ref[0, t, rr] = lbuf[t * cr:(t + 1) * cr, :]


def _attn_prompt(q, k, v, g, batch, seq):
    _, dil = DIL_PATTERNS[g]
    l = seq // dil
    if dil == 1:
        tq, n_chunks, n_res = min(l, 512), 1, 1
        view = (batch, l // tq, 1, tq)
        block = (1, 1, 1, tq)
        index = lambda b, r, n: (b, n, 0, 0, 0)
    else:
        tq, n_chunks, n_res = l, seq // PERM_TILE, max(1, ATTN_STEP_ROWS // l)
        view = (batch, n_chunks, dil, PERM_TILE // dil)
        block = (1, n_chunks, n_res, PERM_TILE // dil)
        index = lambda b, r, n: (b, 0, r, 0, 0)
    nq = l // tq
    has_prev = nq > 1

    def spec(width):
        return pl.BlockSpec(block + (width,), index)

    in_specs = [spec(ATT_W)] * 3
    args = [a.reshape(view + (ATT_W,)) for a in (q, k, v)]
    if has_prev:
        sub = tq // BAND
        prev_spec = pl.BlockSpec((1, 1, 1, BAND, ATT_W),
                                 lambda b, r, n: (b, jnp.maximum(n * sub - 1, 0), 0, 0, 0))
        in_specs += [prev_spec] * 2
        args += [a.reshape(batch, l // BAND, 1, BAND, ATT_W) for a in (k, v)]
    kernel = functools.partial(_attn_prompt_kernel, tq=tq, n_chunks=n_chunks, has_prev=has_prev, n_res=n_res)
    o, lse = pl.pallas_call(
        kernel,
        grid=(batch, dil // n_res, nq),
        in_specs=in_specs,
        out_specs=[spec(ATT_W), spec(LANES)],
        out_shape=[jax.ShapeDtypeStruct(view + (ATT_W,), F32), jax.ShapeDtypeStruct(view + (LANES,), F32)],
        scratch_shapes=[pltpu.VMEM((n_res, tq, ATT_W), BF16), pltpu.VMEM((n_res, BAND + tq, ATT_W), BF16),
                        pltpu.VMEM((n_res, BAND + tq, ATT_W), BF16), pltpu.VMEM((n_res, tq, ATT_W), F32),
                        pltpu.VMEM((n_res, tq, LANES), F32)],
        compiler_params=_cparams(("arbitrary", "arbitrary", "arbitrary")),
        name=f"attn_prompt_g{g}",
    )(*args)
    return o.reshape(batch * seq, ATT_W), lse.reshape(batch * seq, LANES)


def _combine_proj_kernel(o0_ref, l0_ref, o1_ref, l1_ref, o2_ref, l2_ref, s1_ref, s2_ref,
                         w_ref, x_ref, out_ref, *, rs):
    tm = x_ref.shape[0]
    for r in range(tm // rs):
        rows = slice(r * rs, (r + 1) * rs)
        os_ = [o0_ref[rows, :], _select_rows(s1_ref[...], o1_ref[rows, :]),
               _select_rows(s2_ref[...], o2_ref[rows, :])]
        ls = [l0_ref[rows, :], _select_rows(s1_ref[...], l1_ref[rows, :]),
              _select_rows(s2_ref[...], l2_ref[rows, :])]
        top = functools.reduce(jnp.maximum, ls)
        es = [jnp.exp(l - top) for l in ls]
        tot = _sum(es)
        alphas = [e / tot for e in es]
        heads = []
        for h in range(HEADS_A):
            cols = slice(h * HEAD_DIM, (h + 1) * HEAD_DIM)
            heads.append(_sum([a[:, h:h + 1] * o[:, cols] for a, o in zip(alphas, os_)]))
        comb = jnp.concatenate(heads, axis=1).astype(BF16)
        out_ref[rows, :] = x_ref[rows, :] + _dot(comb, w_ref[...])


def _combine_proj(os_, ls, sels, w_all, layer, x, tm, rs):
    m, d = x.shape
    o_spec = pl.BlockSpec((tm, ATT_W), lambda i: (i, 0))
    l_spec = pl.BlockSpec((tm, LANES), lambda i: (i, 0))
    sel_spec = pl.BlockSpec((rs, rs), lambda i: (0, 0))
    x_spec = pl.BlockSpec((tm, d), lambda i: (i, 0))
    kernel = functools.partial(_combine_proj_kernel, rs=rs)
    return pl.pallas_call(
        kernel,
        grid=(m // tm,),
        in_specs=[o_spec, l_spec, o_spec, l_spec, o_spec, l_spec, sel_spec, sel_spec,
                  _layer_spec((ATT_W, d), lambda i: (0, 0), layer), x_spec],
        out_specs=x_spec,
        out_shape=jax.ShapeDtypeStruct((m, d), F32),
        compiler_params=_cparams(("arbitrary",)),
        name="attn_combine_proj",
    )(os_[0], ls[0], os_[1], ls[1], os_[2], ls[2], sels[0], sels[1], w_all, x)


def _attn_sample_kernel(q_ref, kvn0_ref, kvn1_ref, kvn2_ref, c0_ref, c1_ref, c2_ref, o_ref, *, t_new):
    scale = HEAD_DIM ** -0.5
    caches = (c0_ref, c1_ref, c2_ref)
    news = (kvn0_ref, kvn1_ref, kvn2_ref)
    key_u = lax.broadcasted_iota(jnp.int32, (BAND, HEADS_A, 1), 0)

    for t in range(t_new):
        o_g, lse_g = [], []
        for g, (_, dil) in enumerate(DIL_PATTERNS):
            q = q_ref[g, 0, t]
            res = 0 if dil == 1 else t
            s_c = jnp.sum(caches[g][:, res, 0] * q[None], axis=-1, keepdims=True) * scale
            if dil == 1:
                s_c = jnp.where(key_u >= t, s_c, NEG)
                new_ts = list(range(t + 1))
            else:
                new_ts = [t]
            s_n = [jnp.sum(q * news[g][0, t2, 0], axis=-1, keepdims=True) * scale for t2 in new_ts]
            mx = functools.reduce(jnp.maximum, s_n, jnp.max(s_c, axis=0))
            p_c = jnp.exp(s_c - mx[None])
            den = jnp.sum(p_c, axis=0)
            acc = jnp.sum(p_c * caches[g][:, res, 1], axis=0)
            for t2, s in zip(new_ts, s_n):
                p_n = jnp.exp(s - mx)
                den = den + p_n
                acc = acc + p_n * news[g][0, t2, 1]
            o_g.append(acc / den)
            lse_g.append(mx + jnp.log(den))
        top = functools.reduce(jnp.maximum, lse_g)
        es = [jnp.exp(l - top) for l in lse_g]
        tot = _sum(es)
        o_ref[0, t] = _sum([(e / tot) * o for e, o in zip(es, o_g)])


def _attn_sample(q, kv_new, caches, layer, batch, t_new):
    tile = (HEADS_A, HEAD_DIM)
    in_specs = [pl.BlockSpec((N_GROUPS, 1, t_new) + tile, lambda b: (0, b, 0, 0, 0))]
    in_specs += [pl.BlockSpec((1, t_new, 2) + tile, lambda b: (b, 0, 0, 0, 0))] * N_GROUPS
    args = [q] + list(kv_new)
    for (win, dil), c in zip(DIL_PATTERNS, caches):
        n_res = min(dil, t_new)
        in_specs.append(pl.BlockSpec((None, None, BAND, n_res, 2) + tile,
                                     lambda b: (layer, b, 0, 0, 0, 0, 0)))
        args.append(c.reshape(c.shape[:2] + (BAND, dil, 2) + tile))
    kernel = functools.partial(_attn_sample_kernel, t_new=t_new)
    return pl.pallas_call(
        kernel,
        grid=(batch,),
        in_specs=in_specs,
        out_specs=pl.BlockSpec((1, t_new) + tile, lambda b: (b, 0, 0, 0)),
        out_shape=jax.ShapeDtypeStruct((batch, t_new) + tile, F32),
        compiler_params=_cparams(("arbitrary",)),
        name="attn_sample",
    )(*args)


def _matmul_residual_kernel(a_ref, w_ref, x_ref, o_ref):
    o_ref[...] = x_ref[...] + _dot(a_ref[...].astype(BF16), w_ref[...])


def _matmul_residual(a, w_all, layer, x, tm):
    m, k = a.shape
    n = w_all.shape[-1]
    tn = PROJ_COL_TILE
    return pl.pallas_call(
        _matmul_residual_kernel,
        grid=(m // tm, n // tn),
        in_specs=[
            pl.BlockSpec((tm, k), lambda i, j: (i, 0)),
            _layer_spec((k, tn), lambda i, j: (0, j), layer),
            pl.BlockSpec((tm, tn), lambda i, j: (i, j)),
        ],
        out_specs=pl.BlockSpec((tm, tn), lambda i, j: (i, j)),
        out_shape=jax.ShapeDtypeStruct((m, n), F32),
        compiler_params=_cparams(("arbitrary", "arbitrary")),
        name="matmul_residual",
    )(a, w_all, x)


def _gate_act(pre, is_input_gate):
    capped = GATE_CAP * jnp.tanh(pre / GATE_CAP)
    return jnp.where(is_input_gate, capped, _log_sigmoid(capped))


def _mlstm_in_kernel(x_ref, g_ref, w_ref, wg_ref, bg_ref, bgt_ref,
                     qkv_ref, og_ref, gates_ref, gatest_ref, h_scr, *, k_tiles, n_qkv_tiles, k_scale, n_gates):
    j = pl.program_id(1)

    @pl.when(j == 0)
    def _():
        h = _rms_rows(x_ref[...], g_ref[...]).astype(BF16)
        h_scr[...] = h
        pre = _dot_nt(h, wg_ref[...]) + bg_ref[...]
        lane = lax.broadcasted_iota(jnp.int32, pre.shape, 1)
        gates_ref[...] = jnp.where(lane < n_gates, _gate_act(pre, lane < n_gates // 2), 0.0)
        pre_t = _dot_nt(wg_ref[0:n_gates, :], h) + bgt_ref[...]
        row = lax.broadcasted_iota(jnp.int32, pre_t.shape, 0)
        gatest_ref[...] = _gate_act(pre_t, row < n_gates // 2)

    y = _dot_nt(h_scr[...], w_ref[...])

    @pl.when(j < n_qkv_tiles)
    def _():
        is_k = (j >= k_tiles[0]) & (j < k_tiles[1])
        qkv_ref[...] = (y * jnp.where(is_k, k_scale, 1.0).astype(F32)).astype(BF16)

    @pl.when(j >= n_qkv_tiles)
    def _():
        og_ref[...] = y


def _mlstm_in_proj(x, gains, layer, wt_all, ib, bg_all, bgt_all, tm, dqk, dv):
    m, d = x.shape
    tn = PROJ_COL_TILE
    qk_w, v_w = HEADS_B * dqk, HEADS_B * dv
    n_main = 2 * qk_w + 2 * v_w
    n_gates = wt_all.shape[1] - n_main
    n_qkv_tiles = (2 * qk_w + v_w) // tn
    nj = n_main // tn
    kernel = functools.partial(_mlstm_in_kernel, k_tiles=(qk_w // tn, 2 * qk_w // tn),
                               n_qkv_tiles=n_qkv_tiles, k_scale=dqk ** -0.5, n_gates=n_gates)
    return pl.pallas_call(
        kernel,
        grid=(m // tm, nj),
        in_specs=[
            pl.BlockSpec((tm, d), lambda i, j: (i, 0)),
            _layer_spec((1, d), lambda i, j: (0, 0), layer),
            _layer_spec((tn, d), lambda i, j: (j, 0), ib),
            _layer_spec((LANES, d), lambda i, j: (n_main // LANES, 0), ib),
            _layer_spec((1, LANES), lambda i, j: (0, 0), ib),
            _layer_spec((n_gates, 1), lambda i, j: (0, 0), ib),
        ],
        out_specs=[
            pl.BlockSpec((tm, tn), lambda i, j: (i, jnp.minimum(j, n_qkv_tiles - 1))),
            pl.BlockSpec((tm, tn), lambda i, j: (i, jnp.maximum(j - n_qkv_tiles, 0))),
            pl.BlockSpec((tm, LANES), lambda i, j: (i, 0)),
            pl.BlockSpec((n_gates, tm), lambda i, j: (0, i)),
        ],
        out_shape=[
            jax.ShapeDtypeStruct((m, 2 * qk_w + v_w), BF16),
            jax.ShapeDtypeStruct((m, v_w), F32),
            jax.ShapeDtypeStruct((m, LANES), F32),
            jax.ShapeDtypeStruct((n_gates, m), F32),
        ],
        scratch_shapes=[pltpu.VMEM((tm, d), BF16)],
        compiler_params=_cparams(("arbitrary", "arbitrary")),
        name="mlstm_in_proj",
    )(x, gains, wt_all, wt_all, bg_all, bgt_all)


def _mlstm_core_kernel(*refs, chunk, dqk, dv, has_state):
    q_ref, k_ref, v_ref, og_ref, gates_ref, gatest_ref, gh_ref = refs[:7]
    rest = refs[7:]
    if has_state:
        c0_ref, n0_ref, m0_ref = rest[:3]
        rest = rest[3:]
    out_ref, c_out, n_out, m_out, c_scr, n_scr, m_scr = rest
    c = pl.program_id(1)
    n_heads = c_scr.shape[0]

    @pl.when(c == 0)
    def _():
        if has_state:
            c_scr[...] = c0_ref[0]
            n_scr[...] = n0_ref[0]
            m_scr[...] = m0_ref[0]
        else:
            c_scr[...] = jnp.zeros_like(c_scr)
            n_scr[...] = jnp.zeros_like(n_scr)
            m_scr[...] = jnp.zeros_like(m_scr)

    row = lax.broadcasted_iota(jnp.int32, (chunk, chunk), 0)
    col = lax.broadcasted_iota(jnp.int32, (chunk, chunk), 1)
    causal = col <= row
    tri_lo = jnp.where(causal, 1.0, 0.0).astype(BF16)
    tri_up = jnp.where(row <= col, 1.0, 0.0).astype(BF16)

    gates = gates_ref[0]
    gates_t = gatest_ref[0] if len(gatest_ref.shape) == 3 else gatest_ref[...]
    b_cols = _sum([_dot(tri_lo, p) for p in _split3(gates)])
    b_rows = _sum([_dot(p, tri_up) for p in _split3(gates_t)])

    heads = range(n_heads)
    q = [q_ref[0, :, hh * dqk:(hh + 1) * dqk] for hh in heads]
    k = [k_ref[0, :, hh * dqk:(hh + 1) * dqk] for hh in heads]
    v = [v_ref[0, :, hh * dv:(hh + 1) * dv] for hh in heads]
    ig_col = [gates[:, hh:hh + 1] for hh in heads]
    b_col = [b_cols[:, n_heads + hh:n_heads + hh + 1] for hh in heads]
    m_prev = [m_scr[hh][:, 0:1] for hh in heads]

    inter, m_t, pexp = [], [], []
    for hh in heads:
        a_row = gates_t[hh:hh + 1, :] - b_rows[n_heads + hh:n_heads + hh + 1, :]
        a_mask = jnp.where(causal, a_row, NEG)
        a_max = jnp.max(_lane_fold(a_mask, jnp.maximum), axis=-1, keepdims=True)
        inter.append(b_col[hh] + m_prev[hh])
        m_t.append(jnp.maximum(inter[hh], b_col[hh] + a_max))
        pexp.append(jnp.exp(a_mask + (b_col[hh] - m_t[hh])))
    s = [_dot_nt(q[hh], k[hh]) for hh in heads]
    qc = [_dot_nt(q[hh], c_scr[hh].astype(BF16)) for hh in heads]
    p = [pexp[hh] * s[hh] for hh in heads]
    pv = [_dot(p[hh].astype(BF16), v[hh]) for hh in heads]
    for hh in heads:
        w_inter = jnp.exp(inter[hh] - m_t[hh])
        num = w_inter * qc[hh] + pv[hh]
        nq = jnp.sum(q[hh].astype(F32) * n_scr[hh], axis=-1, keepdims=True)
        den = w_inter * nq + jnp.sum(_lane_fold(p[hh], jnp.add), axis=-1, keepdims=True)
        r = 1.0 / jnp.maximum(jnp.abs(den), jnp.exp(-m_t[hh]))
        ms_num = jnp.sum(_lane_fold(num * num, jnp.add), axis=-1, keepdims=True) / dv
        scale = r * lax.rsqrt(r * r * ms_num + EPS)
        cols = slice(hh * dv, (hh + 1) * dv)
        gated = num * scale * gh_ref[:, cols] * jax.nn.sigmoid(og_ref[0, :, cols])
        out_ref[0, :, cols] = gated.astype(BF16)

    w_c, vw, w_s = [], [], []
    for hh in heads:
        b_last = b_col[hh][chunk - 1:chunk, :]
        dec = b_last - b_col[hh] + ig_col[hh]
        m_new = jnp.maximum(b_last + m_prev[hh], jnp.max(dec, axis=0, keepdims=True))
        w_c.append(jnp.exp(b_last + m_prev[hh] - m_new))
        w_s.append(jnp.exp(dec - m_new))
        vw.append((v[hh].astype(F32) * w_s[hh]).astype(BF16))
        m_scr[hh] = jnp.broadcast_to(m_new, (1, LANES))
    upd = [_dot_tn(vw[hh], k[hh]) for hh in heads]
    for hh in heads:
        c_scr[hh] = w_c[hh] * c_scr[hh] + upd[hh]
        n_scr[hh] = w_c[hh] * n_scr[hh] + jnp.sum(k[hh].astype(F32) * w_s[hh], axis=0, keepdims=True)

    @pl.when(c == pl.num_programs(1) - 1)
    def _():
        c_out[0] = c_scr[...]
        n_out[0] = n_scr[...]
        m_out[0] = m_scr[...]


def _mlstm_core(qkv, og, gates, gates_t, gh_all, ib, batch, seq, chunk, dqk, dv, state):
    n_heads = HEADS_B
    qk_w, v_w = n_heads * dqk, n_heads * dv
    nc = seq // chunk
    qkv3 = qkv.reshape(batch, seq, 2 * qk_w + v_w)
    state_specs = [
        pl.BlockSpec((1, n_heads, dv, dqk), lambda b, c: (b, 0, 0, 0)),
        pl.BlockSpec((1, n_heads, 1, dqk), lambda b, c: (b, 0, 0, 0)),
        pl.BlockSpec((1, n_heads, 1, LANES), lambda b, c: (b, 0, 0, 0)),
    ]
    in_specs = [
        pl.BlockSpec((1, chunk, qk_w), lambda b, c: (b, c, 0)),
        pl.BlockSpec((1, chunk, qk_w), lambda b, c: (b, c, 1)),
        pl.BlockSpec((1, chunk, v_w), lambda b, c: (b, c, 2 * qk_w // v_w)),
        pl.BlockSpec((1, chunk, v_w), lambda b, c: (b, c, 0)),
        pl.BlockSpec((1, chunk, LANES), lambda b, c: (b, c, 0)),
        pl.BlockSpec((2 * n_heads, chunk), lambda b, c: (0, b * nc + c)) if gates_t.ndim == 2
        else pl.BlockSpec((1, 2 * n_heads, chunk), lambda b, c: (b * nc + c, 0, 0)),
        _layer_spec((1, v_w), lambda b, c: (0, 0), ib),
    ]
    args = [qkv3, qkv3, qkv3, og.reshape(batch, seq, v_w), gates.reshape(batch, seq, LANES), gates_t, gh_all]
    if state is not None:
        in_specs += state_specs
        args += list(state)
    kernel = functools.partial(_mlstm_core_kernel, chunk=chunk, dqk=dqk, dv=dv,
                               has_state=state is not None)
    return pl.pallas_call(
        kernel,
        grid=(batch, nc),
        in_specs=in_specs,
        out_specs=[pl.BlockSpec((1, chunk, v_w), lambda b, c: (b, c, 0))] + state_specs,
        out_shape=[
            jax.ShapeDtypeStruct((batch, seq, v_w), BF16),
            jax.ShapeDtypeStruct((batch, n_heads, dv, dqk), F32),
            jax.ShapeDtypeStruct((batch, n_heads, 1, dqk), F32),
            jax.ShapeDtypeStruct((batch, n_heads, 1, LANES), F32),
        ],
        scratch_shapes=[pltpu.VMEM((n_heads, dv, dqk), F32), pltpu.VMEM((n_heads, 1, dqk), F32),
                        pltpu.VMEM((n_heads, 1, LANES), F32)],
        compiler_params=_cparams(("arbitrary", "arbitrary")),
        name="mlstm_core",
    )(*args)


def _ffn_kernel(*refs, rs, tiles_per_seq, t_new, cast_next):
    x_ref, g_ref, wa_ref, wb_ref, cwa_ref, cwb_ref, cba_ref, cbb_ref, wd_ref = refs[:9]
    rest = refs[9:]
    sample = t_new is not None
    if sample:
        s1a_ref, s1b_ref, s2a_ref, s2b_ref, o_ref, ua_ref, ub_ref, h_scr = rest
        s1_refs, s2_refs, u_refs = (s1a_ref, s1b_ref), (s2a_ref, s2b_ref), (ua_ref, ub_ref)
    else:
        if cast_next:
            up_next_ref, down_next_ref, o_ref, tail_ref, up_cast_ref, down_cast_ref, h_scr, tail_scr = rest
            up_cast_ref[...] = up_next_ref[...].astype(BF16)
            down_cast_ref[...] = down_next_ref[...].astype(BF16)
        else:
            o_ref, tail_ref, h_scr, tail_scr = rest
    i = pl.program_id(0)
    f = pl.program_id(1)
    tm = x_ref.shape[0]
    n_sub = tm // rs

    @pl.when(f == 0)
    def _():
        x = x_ref[...]
        h_scr[...] = _rms_rows(x, g_ref[...]).astype(BF16)
        o_ref[...] = x

    parts = ((wa_ref, cwa_ref, cba_ref), (wb_ref, cwb_ref, cbb_ref))
    if not sample:
        @pl.when((i == 0) & (f == 0))
        def _():
            tail_scr[...] = jnp.zeros_like(tail_scr)

        seq_start = i % tiles_per_seq == 0
        prev = []
        for part in range(2):
            t = jnp.where(seq_start, 0.0, tail_scr[f, part])
            prev.append((t[SUBLANES - 2:SUBLANES - 1, :], t[SUBLANES - 1:SUBLANES, :]))

    def up(r):
        h = h_scr[r * rs:(r + 1) * rs, :]
        return [_dot(h, w_ref[...]) for w_ref, _, _ in parts]

    def conv_gate(r, us):
        conv = []
        for part, (_, cw_ref, cb_ref) in enumerate(parts):
            u = us[part]
            ridx = lax.broadcasted_iota(jnp.int32, u.shape, 0)
            r1 = pltpu.roll(u, 1, 0)
            r2 = pltpu.roll(u, 2, 0)
            if sample:
                u_refs[part][...] = u
                pos = ridx % t_new
                u1 = jnp.where(pos >= 1, r1, s1_refs[part][...])
                u2 = jnp.where(pos >= 2, r2, s2_refs[part][...])
            else:
                p2, p1 = prev[part]
                u1 = jnp.where(ridx == 0, p1, r1)
                u2 = jnp.where(ridx == 0, p2, jnp.where(ridx == 1, p1, r2))
                prev[part] = (u[rs - 2:rs - 1, :], u[rs - 1:rs, :])
                if r == n_sub - 1:
                    tail = u[rs - SUBLANES:rs, :]
                    tail_scr[f, part] = tail
                    tail_ref[0, part] = tail
            cw = cw_ref[...]
            conv.append(cb_ref[...] + cw[0:1, :] * u2 + cw[1:2, :] * u1 + cw[2:3, :] * u)
        return (jax.nn.silu(conv[0]) * conv[1]).astype(BF16)

    us = up(0)
    for r in range(n_sub):
        us_next = up(r + 1) if r + 1 < n_sub else None
        z = conv_gate(r, us)
        o_ref[r * rs:(r + 1) * rs, :] += _dot(z, wd_ref[...])
        us = us_next


def _conv_ffn(x, gains, layer, w_up_all, conv_w_all, conv_b_all, w_down_all, w_layer, tm, rs,
              seq=None, sample_state=None, t_new=None, cast_next=None):
    m, d = x.shape
    d_ff = w_down_all.shape[1]
    tf = FFN_COL_TILE
    nf = d_ff // tf
    sample = sample_state is not None
    in_specs = [
        pl.BlockSpec((tm, d), lambda i, f: (i, 0)),
        _layer_spec((1, d), lambda i, f: (0, 0), layer),
        _layer_spec((d, tf), lambda i, f: (0, f), w_layer),
        _layer_spec((d, tf), lambda i, f: (0, nf + f), w_layer),
        _layer_spec((CONV_W, tf), lambda i, f: (0, f), layer),
        _layer_spec((CONV_W, tf), lambda i, f: (0, nf + f), layer),
        _layer_spec((1, tf), lambda i, f: (0, f), layer),
        _layer_spec((1, tf), lambda i, f: (0, nf + f), layer),
        _layer_spec((tf, d), lambda i, f: (f, 0), w_layer),
    ]
    args = [x, gains, w_up_all, w_up_all, conv_w_all, conv_w_all, conv_b_all, conv_b_all, w_down_all]
    scratch = [pltpu.VMEM((tm, d), BF16)]
    if sample:
        tiles_per_seq = None
        half_specs = [pl.BlockSpec((tm, tf), lambda i, f: (i, f)), pl.BlockSpec((tm, tf), lambda i, f: (i, nf + f))]
        in_specs += half_specs * 2
        args += [sample_state[0]] * 2 + [sample_state[1]] * 2
        out_specs = [pl.BlockSpec((tm, d), lambda i, f: (i, 0))] + [pl.BlockSpec((tm, tf), lambda i, f: (i, f))] * 2
        out_shape = [jax.ShapeDtypeStruct((m, d), F32)] + [jax.ShapeDtypeStruct((m, d_ff), F32)] * 2
    else:
        tiles_per_seq = seq // tm
        out_specs = [pl.BlockSpec((tm, d), lambda i, f: (i, 0)),
                     pl.BlockSpec((1, 2, SUBLANES, tf), lambda i, f: (i, 0, 0, f))]
        out_shape = [jax.ShapeDtypeStruct((m, d), F32),
                     jax.ShapeDtypeStruct((m // tm, 2, SUBLANES, d_ff), F32)]
        scratch.append(pltpu.VMEM((nf, 2, SUBLANES, tf), F32))
        if cast_next is not None:
            up_f32, down_f32, next_layer = cast_next
            n_steps = (m // tm) * nf
            cw, rw = 2 * d_ff // n_steps, d_ff // n_steps
            assert cw % LANES == 0 and rw % SUBLANES == 0 and cw * n_steps == 2 * d_ff and rw * n_steps == d_ff
            in_specs += [_layer_spec((d, cw), lambda i, f: (0, i * nf + f), next_layer),
                         _layer_spec((rw, d), lambda i, f: (i * nf + f, 0), next_layer)]
            args += [up_f32, down_f32]
            out_specs += [_layer_spec((d, cw), lambda i, f: (0, i * nf + f), 0),
                          _layer_spec((rw, d), lambda i, f: (i * nf + f, 0), 0)]
            out_shape += [jax.ShapeDtypeStruct((1, d, 2 * d_ff), BF16), jax.ShapeDtypeStruct((1, d_ff, d), BF16)]
    kernel = functools.partial(_ffn_kernel, rs=rs, tiles_per_seq=tiles_per_seq, t_new=t_new,
                               cast_next=cast_next is not None)
    return pl.pallas_call(
        kernel,
        grid=(m // tm, nf),
        in_specs=in_specs,
        out_specs=out_specs,
        out_shape=out_shape,
        scratch_shapes=scratch,
        compiler_params=_cparams(("arbitrary", "arbitrary")),
        name="conv_ffn_sample" if sample else "conv_ffn",
    )(*args)


def _shift_append(cache, new):
    n_buf, n_new = cache.shape[2], new.shape[2]
    zero = jnp.zeros((), cache.dtype)
    no_pad = (0, 0, 0)
    shifted = lax.pad(cache, zero, [no_pad, no_pad, (-n_new, n_new, 0)] + [no_pad] * (cache.ndim - 3))
    tail = lax.pad(new, zero, [no_pad, no_pad, (n_buf - n_new, 0, 0)] + [no_pad] * (cache.ndim - 3))
    row = lax.broadcasted_iota(jnp.int32, cache.shape, 2)
    return jnp.where(row < n_buf - n_new, shifted, tail)


def _rope_tables(pos):
    half = HEAD_DIM // 2
    inv_freq = ROPE_THETA ** (-jnp.arange(half, dtype=F32) / half)
    ang = pos.astype(F32)[:, None] * inv_freq[None, :]
    cos, sin = jnp.cos(ang), jnp.sin(ang)
    return jnp.concatenate([cos, cos], axis=-1), jnp.concatenate([-sin, sin], axis=-1)


def _perm_matrices(dil):
    pos = np.arange(PERM_TILE)
    slot = (pos % dil) * (PERM_TILE // dil) + pos // dil
    to_residue_major = np.zeros((PERM_TILE, PERM_TILE), np.float32)
    to_residue_major[slot, pos] = 1.0
    return jnp.asarray(to_residue_major, BF16), jnp.asarray(to_residue_major.T, BF16)


def kernel(x_prompt, x_sample, cache_kv_w128, cache_kv_w512, cache_kv_w2048, state_mlstm_C, state_mlstm_n, state_mlstm_m, state_ffn_conv, norm_mix, norm_ffn, attn_w_qkv, attn_q_norm, attn_k_norm, attn_w_o, mlstm_w_in, mlstm_b_gates, mlstm_norm_h, mlstm_w_out, ffn_w_up, ffn_conv_w, ffn_conv_b, ffn_w_down):
    batch, seq, d = x_prompt.shape
    dec_batch, t_new, _ = x_sample.shape
    depth = norm_mix.shape[0]
    caches = (cache_kv_w128, cache_kv_w512, cache_kv_w2048)
    dqk = state_mlstm_C.shape[-1]
    dv = state_mlstm_C.shape[-2]
    d_ff = ffn_w_down.shape[1]
    ms = dec_batch * t_new
    tm_p = ROW_TILE
    tile = (HEADS_A, HEAD_DIM)

    xp = x_prompt.reshape(batch * seq, d)
    xs = x_sample.reshape(ms, d)

    w_qkv, w_o, w_out = (a.astype(BF16) for a in (attn_w_qkv, attn_w_o, mlstm_w_out))
    w_up, w_down = ffn_w_up[:1].astype(BF16), ffn_w_down[:1].astype(BF16)
    g_mix = norm_mix[:, None, :]
    g_ffn = norm_ffn[:, None, :]
    conv_b = ffn_conv_b[:, None, :]
    n_main = 2 * HEADS_B * dqk + 2 * HEADS_B * dv
    n_gates = mlstm_b_gates.shape[-1]
    w_in_t = jnp.swapaxes(mlstm_w_in, 1, 2).astype(BF16)
    bg_row = jnp.zeros((mlstm_b_gates.shape[0], 1, LANES), F32).at[:, 0, :n_gates].set(mlstm_b_gates)
    bg_col = mlstm_b_gates[:, :, None]
    gh = mlstm_norm_h[:, None, :]

    cos_p, sin_p = _rope_tables(jnp.arange(seq))
    cos_s, sin_s = _rope_tables(PAST_LEN + jnp.arange(t_new))
    cos_s, sin_s = jnp.tile(cos_s, (dec_batch, 1)), jnp.tile(sin_s, (dec_batch, 1))
    perms = [None] + [_perm_matrices(dil) for _, dil in DIL_PATTERNS[1:]]

    kv_p = [None] * N_GROUPS
    n_attn_layers = (depth + 1) // 2
    kv_new = [[] for _ in DIL_PATTERNS]
    c_p, n_p, m_p, c_s, n_s, m_s = [], [], [], [], [], []
    conv_p, conv_s = [], []

    for layer in range(depth):
        if layer % 2 == 0:
            ia = layer // 2
            gains = jnp.stack([attn_q_norm[ia], attn_k_norm[ia]])
            h = _norm(xp, g_mix, layer, tm_p)
            os_, ls = [], []
            for g, (win, dil) in enumerate(DIL_PATTERNS):
                perm = None if perms[g] is None else perms[g][0]
                q, k, v, kv_p[g] = _group_proj(h, w_qkv, ia, g, gains, cos_p, sin_p, perm,
                                               ATTN_ROW_TILE, ATTN_ROW_TILE,
                                               kv_layers=n_attn_layers, kv_stack=kv_p[g])
                o_g, l_g = _attn_prompt(q, k, v, g, batch, seq)
                os_.append(o_g)
                ls.append(l_g)
            xp = _combine_proj(os_, ls, [perms[1][1], perms[2][1]], w_o, ia, xp, ATTN_ROW_TILE, ROW_SUBTILE)
            h = _norm(xs, g_mix, layer, ms)
            qs, news = [], []
            for g in range(N_GROUPS):
                q, _, _, kvf = _group_proj(h, w_qkv, ia, g, gains, cos_s, sin_s, None, ms, ms)
                qs.append(q.astype(F32).reshape((dec_batch, t_new) + tile))
                news.append(kvf.reshape((dec_batch, t_new, 2) + tile))
                kv_new[g].append(news[-1])
            comb_s = _attn_sample(jnp.stack(qs), news, caches, ia, dec_batch, t_new)
            xs = _matmul_residual(comb_s.reshape(ms, ATT_W), w_o, ia, xs, ms)
        else:
            ib = layer // 2
            qkv, og, gates, gates_t = _mlstm_in_proj(xp, g_mix, layer, w_in_t, ib, bg_row, bg_col,
                                                     tm_p, dqk, dv)
            gated, c_f, n_f, m_f = _mlstm_core(qkv, og, gates, gates_t, gh, ib, batch, seq,
                                               MLSTM_CHUNK, dqk, dv, None)
            xp = _matmul_residual(gated.reshape(batch * seq, HEADS_B * dv), w_out, ib, xp, tm_p)
            c_p.append(c_f)
            n_p.append(n_f[:, :, 0, :])
            m_p.append(m_f[:, :, 0, 0])
            qkv, og, gates, _ = _mlstm_in_proj(xs, g_mix, layer, w_in_t, ib, bg_row, bg_col, ms, dqk, dv)
            tp = SAMPLE_PAD_T
            n_pad = tp - t_new

            def front_pad(a):
                a = a.reshape(dec_batch, t_new, a.shape[-1])
                a = jnp.pad(a, ((0, 0), (n_pad, 0), (0, 0)))
                return a.reshape(dec_batch * tp, a.shape[-1])

            lane_is_ig = (jnp.arange(LANES) < HEADS_B)[None, None, :]
            pad_gates = jnp.broadcast_to(jnp.where(lane_is_ig, NEG, 0.0).astype(F32), (dec_batch, n_pad, LANES))
            g3 = jnp.concatenate([pad_gates, gates.reshape(dec_batch, t_new, LANES)], axis=1)
            gates_t_pad = jnp.swapaxes(g3[:, :, :n_gates], 1, 2)
            state = (state_mlstm_C[ib], state_mlstm_n[ib][:, :, None, :],
                     jnp.broadcast_to(state_mlstm_m[ib][:, :, None, None], (dec_batch, HEADS_B, 1, LANES)))
            gated, c_f, n_f, m_f = _mlstm_core(front_pad(qkv), front_pad(og), g3.reshape(dec_batch * tp, LANES),
                                               gates_t_pad, gh, ib, dec_batch, tp, tp, dqk, dv, state)
            gated = gated[:, n_pad:].reshape(ms, HEADS_B * dv)
            xs = _matmul_residual(gated, w_out, ib, xs, ms)
            c_s.append(c_f)
            n_s.append(n_f[:, :, 0, :])
            m_s.append(m_f[:, :, 0, 0])

        cast_next = (ffn_w_up, ffn_w_down, layer + 1) if layer + 1 < depth else None
        xp, tails, *w_next = _conv_ffn(xp, g_ffn, layer, w_up, ffn_conv_w, conv_b, w_down, 0, tm_p,
                                       FFN_ROW_SUBTILE, seq=seq, cast_next=cast_next)
        tiles_per_seq = seq // tm_p
        tails = tails[tiles_per_seq - 1::tiles_per_seq, :, SUBLANES - (CONV_W - 1):, :]
        conv_p.append(jnp.swapaxes(tails, 1, 2).reshape(batch, CONV_W - 1, 2 * d_ff))
        st = state_ffn_conv[layer]
        pad_rows = t_new - (CONV_W - 1)
        s2 = jnp.pad(st, ((0, 0), (0, pad_rows), (0, 0))).reshape(ms, 2 * d_ff)
        s1 = jnp.pad(st[:, 1:], ((0, 0), (0, pad_rows + 1), (0, 0))).reshape(ms, 2 * d_ff)
        xs, u_a, u_b = _conv_ffn(xs, g_ffn, layer, w_up, ffn_conv_w, conv_b, w_down, 0, ms, ms,
                                 sample_state=(s1, s2), t_new=t_new)
        if w_next:
            w_up, w_down = w_next
        u_s = jnp.concatenate([u_a, u_b], axis=-1).reshape(dec_batch, t_new, 2 * d_ff)
        ext = jnp.concatenate([state_ffn_conv[layer], u_s], axis=1)
        conv_s.append(ext[:, t_new:])

    stack = jnp.stack
    kv_s = [_shift_append(c, stack(new)) for c, new in zip(caches, kv_new)]
    kv_p = [a.reshape((n_attn_layers, batch, seq, 2) + tile)[:, :, seq - min(win, seq):]
            for a, (win, _) in zip(kv_p, DIL_PATTERNS)]
    return (xp.reshape(batch, seq, d), xs.reshape(dec_batch, t_new, d),
            kv_p[0], kv_s[0], kv_p[1], kv_s[1], kv_p[2], kv_s[2],
            stack(c_p), stack(c_s), stack(n_p), stack(n_s), stack(m_p), stack(m_s),
            stack(conv_p), stack(conv_s))
```

```python
import functools

import numpy as np
import jax
import jax.numpy as jnp
from jax import lax
from jax.experimental import pallas as pl
from jax.experimental.pallas import tpu as pltpu

F32 = jnp.float32
BF16 = jnp.bfloat16

DIL_PATTERNS = ((128, 1), (512, 4), (2048, 16))
N_GROUPS = len(DIL_PATTERNS)
HEADS_A = 8
HEAD_DIM = 128
ATT_W = HEADS_A * HEAD_DIM
BAND = 128
ROPE_THETA = 10000.0
PAST_LEN = 16384
HEADS_B = 8
GATE_CAP = 15.0
CONV_W = 3
EPS = 1e-6
NEG = -1e30

LANES = 128
SUBLANES = 8
VMEM_LIMIT_BYTES = 60 * 1024 * 1024
ROW_TILE = 1024
ATTN_ROW_TILE = 512
ATTN_STEP_ROWS = 1024
ROW_SUBTILE = 256
PERM_TILE = 256
PROJ_COL_TILE = 1024
FFN_COL_TILE = 512
FFN_ROW_SUBTILE = 512
MLSTM_CHUNK = 512
SAMPLE_PAD_T = 16


def _cparams(sem):
    return pltpu.CompilerParams(dimension_semantics=sem, vmem_limit_bytes=VMEM_LIMIT_BYTES)


def _rms_rows(x, g):
    ms = jnp.mean(x * x, axis=-1, keepdims=True)
    return (x * lax.rsqrt(ms + EPS)) * g


def _log_sigmoid(x):
    return jnp.minimum(x, 0.0) - jnp.log1p(jnp.exp(-jnp.abs(x)))


def _dot(a, b):
    return jnp.dot(a, b, preferred_element_type=F32)


def _dot_nt(a, b):
    return lax.dot_general(a, b, (((1,), (1,)), ((), ())), preferred_element_type=F32)


def _dot_tn(a, b):
    return lax.dot_general(a, b, (((0,), (0,)), ((), ())), preferred_element_type=F32)


def _sum(xs):
    return functools.reduce(lambda a, b: a + b, xs)


def _lane_fold(x, op):
    w = x.shape[-1]
    if w <= LANES or w % LANES:
        return x
    return functools.reduce(op, [x[:, i:i + LANES] for i in range(0, w, LANES)])


def _split3(x):
    hi = x.astype(BF16)
    r = x - hi.astype(F32)
    mid = r.astype(BF16)
    lo = (r - mid.astype(F32)).astype(BF16)
    return hi, mid, lo


def _select_rows(sel, x):
    return _sum([_dot(sel, p) for p in _split3(x)])


def _layer_spec(block, index_map, layer, **kwargs):
    return pl.BlockSpec((None,) + block, lambda *idx: (layer,) + index_map(*idx), **kwargs)


def _norm_kernel(x_ref, g_ref, h_ref):
    h_ref[...] = _rms_rows(x_ref[...], g_ref[...]).astype(BF16)


def _norm(x, gains, layer, tm):
    m, d = x.shape
    return pl.pallas_call(
        _norm_kernel,
        grid=(m // tm,),
        in_specs=[pl.BlockSpec((tm, d), lambda i: (i, 0)),
                  _layer_spec((1, d), lambda i: (0, 0), layer)],
        out_specs=pl.BlockSpec((tm, d), lambda i: (i, 0)),
        out_shape=jax.ShapeDtypeStruct((m, d), BF16),
        compiler_params=_cparams(("arbitrary",)),
        name="rms_norm",
    )(x, gains)


def _group_proj_kernel(*refs, rs, has_perm, has_alias, fuse_norm):
    h_ref, w_ref, gains_ref, cos_ref, sin_ref = refs[:5]
    rest = refs[5:]
    if fuse_norm:
        x_ref, g_ref, rest, h_ref = h_ref, rest[0], rest[1:-1], rest[-1]
        h_ref[...] = _rms_rows(x_ref[...], g_ref[...]).astype(BF16)
    if has_perm:
        perm_ref, rest = rest[0], rest[1:]
    if has_alias:
        rest = rest[1:]
    q_ref, k_ref, v_ref, kv_ref = rest
    kv_slabs = [kv_ref.at[l] for l in range(kv_ref.shape[0])] if len(kv_ref.shape) == 3 else [kv_ref]
    tm = h_ref.shape[0]
    n_sub = tm // rs
    kv_rows = 2 * HEADS_A

    def head_cols(y, kind):
        return [y[:, kind * ATT_W + hh * HEAD_DIM:kind * ATT_W + (hh + 1) * HEAD_DIM] for hh in range(HEADS_A)]

    def rope(heads, gain, cos, sin):
        outs = []
        for y in heads:
            yn = _rms_rows(y, gain)
            outs.append(yn * cos + pltpu.roll(yn, HEAD_DIM // 2, 1) * sin)
        return outs

    def finish(r, y):
        rows = slice(r * rs, (r + 1) * rs)
        cos = cos_ref[rows, :]
        sin = sin_ref[rows, :]
        q = rope(head_cols(y, 0), gains_ref[0:1, :], cos, sin)
        k = rope(head_cols(y, 1), gains_ref[1:2, :], cos, sin)
        v = head_cols(y, 2)
        for kind, heads in enumerate((k, v)):
            for hh in range(HEADS_A):
                for slab in kv_slabs:
                    slab[pl.ds(r * rs * kv_rows + kind * HEADS_A + hh, rs, stride=kv_rows), :] = heads[hh]
        for heads, ref in ((q, q_ref), (k, k_ref), (v, v_ref)):
            val = jnp.concatenate(heads, axis=1).astype(BF16)
            if has_perm:
                pt = perm_ref.shape[0]
                for c in range(rs // pt):
                    piece = _dot(perm_ref[...], val[c * pt:(c + 1) * pt, :]).astype(BF16)
                    ref[r * rs + c * pt:r * rs + (c + 1) * pt, :] = piece
            else:
                ref[rows, :] = val

    y = _dot(h_ref[0:rs, :], w_ref[...])
    for r in range(n_sub):
        y_next = _dot(h_ref[(r + 1) * rs:(r + 2) * rs, :], w_ref[...]) if r + 1 < n_sub else None
        finish(r, y)
        y = y_next


def _group_proj(h, w_all, layer, g, gains, cos, sin, perm, tm, rs, kv_layers=None, kv_stack=None,
                norm=None):
    m, d = h.shape
    n_pos_tiles = cos.shape[0] // tm
    in_specs = [
        pl.BlockSpec((tm, d), lambda i: (i, 0)),
        _layer_spec((d, 3 * ATT_W), lambda i: (0, g), layer, pipeline_mode=pl.Buffered(1)),
        pl.BlockSpec((2, HEAD_DIM), lambda i: (0, 0)),
        pl.BlockSpec((tm, HEAD_DIM), lambda i: (i % n_pos_tiles, 0)),
        pl.BlockSpec((tm, HEAD_DIM), lambda i: (i % n_pos_tiles, 0)),
    ]
    args = [h, w_all, gains, cos, sin]
    if norm is not None:
        in_specs.append(_layer_spec((1, d), lambda i: (0, 0), norm[1]))
        args.append(norm[0])
    if perm is not None:
        in_specs.append(pl.BlockSpec(perm.shape, lambda i: (0, 0)))
        args.append(perm)
    row_spec = pl.BlockSpec((tm, ATT_W), lambda i: (i, 0))
    kv_rows = 2 * HEADS_A
    aliases = {}
    if kv_layers is None:
        kv_spec = pl.BlockSpec((tm * kv_rows, HEAD_DIM), lambda i: (i, 0))
        kv_shape = jax.ShapeDtypeStruct((m * kv_rows, HEAD_DIM), F32)
    else:
        kv_shape = jax.ShapeDtypeStruct((kv_layers, m * kv_rows, HEAD_DIM), F32)
        if kv_stack is None:
            kv_spec = pl.BlockSpec((kv_layers, tm * kv_rows, HEAD_DIM), lambda i: (0, i, 0))
        else:
            kv_spec = _layer_spec((tm * kv_rows, HEAD_DIM), lambda i: (i, 0), layer)
            aliases = {len(args): 3}
            in_specs.append(pl.BlockSpec(memory_space=pl.ANY))
            args.append(kv_stack)
    kernel = functools.partial(_group_proj_kernel, rs=rs, has_perm=perm is not None, has_alias=bool(aliases),
                               fuse_norm=norm is not None)
    extra_specs = [pl.BlockSpec((tm, d), lambda i: (i, 0))] if norm is not None else []
    extra_shape = [jax.ShapeDtypeStruct((m, d), BF16)] if norm is not None else []
    return pl.pallas_call(
        kernel,
        grid=(m // tm,),
        in_specs=in_specs,
        out_specs=[row_spec, row_spec, row_spec, kv_spec] + extra_specs,
        out_shape=[jax.ShapeDtypeStruct((m, ATT_W), BF16)] * 3 + [kv_shape] + extra_shape,
        input_output_aliases=aliases,
        compiler_params=_cparams(("arbitrary",)),
        name=f"attn_proj_g{g}",
    )(*args)


def _attn_prompt_kernel(*refs, tq, n_chunks, has_prev, n_res):
    q_ref, k_ref, v_ref = refs[:3]
    rest = refs[3:]
    if has_prev:
        kp_ref, vp_ref = rest[:2]
        rest = rest[2:]
    o_ref, lse_ref, qbufs, kbufs, vbufs, obufs, lbufs = rest
    n = pl.program_id(2)
    scale = HEAD_DIM ** -0.5
    cr = tq // n_chunks

    qi = lax.broadcasted_iota(jnp.int32, (BAND, 2 * BAND), 0)
    kj = lax.broadcasted_iota(jnp.int32, (BAND, 2 * BAND), 1)
    bias = jnp.where((kj >= qi) & (kj <= qi + BAND), 0.0, NEG).astype(F32)
    first_pen = jnp.where(n == 0, NEG, 0.0).astype(F32) if has_prev else NEG
    bias_first = bias + jnp.where(kj < BAND, first_pen, 0.0)
    lane = lax.broadcasted_iota(jnp.int32, (BAND, LANES), 1)

    for rr in range(n_res):
        qbuf, kbuf, vbuf, obuf, lbuf = (buf.at[rr] for buf in (qbufs, kbufs, vbufs, obufs, lbufs))
        if has_prev:
            kbuf[0:BAND, :] = kp_ref[0, 0, rr]
            vbuf[0:BAND, :] = vp_ref[0, 0, rr]
        else:
            kbuf[0:BAND, :] = jnp.zeros((BAND, ATT_W), BF16)
            vbuf[0:BAND, :] = jnp.zeros((BAND, ATT_W), BF16)
        for t in range(n_chunks):
            qbuf[t * cr:(t + 1) * cr, :] = q_ref[0, t, rr]
            kbuf[BAND + t * cr:BAND + (t + 1) * cr, :] = k_ref[0, t, rr]
            vbuf[BAND + t * cr:BAND + (t + 1) * cr, :] = v_ref[0, t, rr]

        for c in range(tq // BAND):
            rows = slice(c * BAND, (c + 1) * BAND)
            krows = slice(c * BAND, c * BAND + 2 * BAND)
            b = bias_first if c == 0 else bias
            heads = range(HEADS_A)
            cols = [slice(h * HEAD_DIM, (h + 1) * HEAD_DIM) for h in heads]
            s = [_dot_nt(qbuf[rows, cols[h]], kbuf[krows, cols[h]]) * scale + b for h in heads]
            mx = [jnp.max(_lane_fold(s[h], jnp.maximum), axis=-1, keepdims=True) for h in heads]
            p = [jnp.exp(s[h] - mx[h]) for h in heads]
            den = [jnp.sum(_lane_fold(p[h], jnp.add), axis=-1, keepdims=True) for h in heads]
            pv = [_dot(p[h].astype(BF16), vbuf[krows, cols[h]]) for h in heads]
            lse_tile = jnp.zeros((BAND, LANES), F32)
            for h in heads:
                obuf[rows, cols[h]] = pv[h] / den[h]
                lse_tile = jnp.where(lane == h, mx[h] + jnp.log(den[h]), lse_tile)
            lbuf[rows, :] = lse_tile

        for t in range(n_chunks):
            o_ref[0, t, rr] = obuf[t * cr:(t + 1) * cr, :]
            lse_ref[0, t, rr] = lbuf[t * cr:(t + 1) * cr, :]


def _attn_prompt(q, k, v, g, batch, seq):
    _, dil = DIL_PATTERNS[g]
    l = seq // dil
    if dil == 1:
        tq, n_chunks, n_res = min(l, 512), 1, 1
        view = (batch, l // tq, 1, tq)
        block = (1, 1, 1, tq)
        index = lambda b, r, n: (b, n, 0, 0, 0)
    else:
        tq, n_chunks, n_res = l, seq // PERM_TILE, max(1, ATTN_STEP_ROWS // l)
        view = (batch, n_chunks, dil, PERM_TILE // dil)
        block = (1, n_chunks, n_res, PERM_TILE // dil)
        index = lambda b, r, n: (b, 0, r, 0, 0)
    nq = l // tq
    has_prev = nq > 1

    def spec(width):
        return pl.BlockSpec(block + (width,), index)

    in_specs = [spec(ATT_W)] * 3
    args = [a.reshape(view + (ATT_W,)) for a in (q, k, v)]
    if has_prev:
        sub = tq // BAND
        prev_spec = pl.BlockSpec((1, 1, 1, BAND, ATT_W),
                                 lambda b, r, n: (b, jnp.maximum(n * sub - 1, 0), 0, 0, 0))
        in_specs += [prev_spec] * 2
        args += [a.reshape(batch, l // BAND, 1, BAND, ATT_W) for a in (k, v)]
    kernel = functools.partial(_attn_prompt_kernel, tq=tq, n_chunks=n_chunks, has_prev=has_prev, n_res=n_res)
    o, lse = pl.pallas_call(
        kernel,
        grid=(batch, dil // n_res, nq),
        in_specs=in_specs,
        out_specs=[spec(ATT_W), spec(LANES)],
        out_shape=[jax.ShapeDtypeStruct(view + (ATT_W,), F32), jax.ShapeDtypeStruct(view + (LANES,), F32)],
        scratch_shapes=[pltpu.VMEM((n_res, tq, ATT_W), BF16), pltpu.VMEM((n_res, BAND + tq, ATT_W), BF16),
                        pltpu.VMEM((n_res, BAND + tq, ATT_W), BF16), pltpu.VMEM((n_res, tq, ATT_W), F32),
                        pltpu.VMEM((n_res, tq, LANES), F32)],
        compiler_params=_cparams(("arbitrary", "arbitrary", "arbitrary")),
        name=f"attn_prompt_g{g}",
    )(*args)
    return o.reshape(batch * seq, ATT_W), lse.reshape(batch * seq, LANES)


def _combine_proj_kernel(o0_ref, l0_ref, o1_ref, l1_ref, o2_ref, l2_ref, s1_ref, s2_ref,
                         w_ref, x_ref, out_ref, *, rs):
    tm = x_ref.shape[0]
    for r in range(tm // rs):
        rows = slice(r * rs, (r + 1) * rs)
        os_ = [o0_ref[rows, :], _select_rows(s1_ref[...], o1_ref[rows, :]),
               _select_rows(s2_ref[...], o2_ref[rows, :])]
        ls = [l0_ref[rows, :], _select_rows(s1_ref[...], l1_ref[rows, :]),
              _select_rows(s2_ref[...], l2_ref[rows, :])]
        top = functools.reduce(jnp.maximum, ls)
        es = [jnp.exp(l - top) for l in ls]
        tot = _sum(es)
        alphas = [e / tot for e in es]
        heads = []
        for h in range(HEADS_A):
            cols = slice(h * HEAD_DIM, (h + 1) * HEAD_DIM)
            heads.append(_sum([a[:, h:h + 1] * o[:, cols] for a, o in zip(alphas, os_)]))
        comb = jnp.concatenate(heads, axis=1).astype(BF16)
        out_ref[rows, :] = x_ref[rows, :] + _dot(comb, w_ref[...])


def _combine_proj(os_, ls, sels, w_all, layer, x, tm, rs):
    m, d = x.shape
    o_spec = pl.BlockSpec((tm, ATT_W), lambda i: (i, 0))
    l_spec = pl.BlockSpec((tm, LANES), lambda i: (i, 0))
    sel_spec = pl.BlockSpec((rs, rs), lambda i: (0, 0))
    x_spec = pl.BlockSpec((tm, d), lambda i: (i, 0))
    kernel = functools.partial(_combine_proj_kernel, rs=rs)
    return pl.pallas_call(
        kernel,
        grid=(m // tm,),
        in_specs=[o_spec, l_spec, o_spec, l_spec, o_spec, l_spec, sel_spec, sel_spec,
                  _layer_spec((ATT_W, d), lambda i: (0, 0), layer), x_spec],
        out_specs=x_spec,
        out_shape=jax.ShapeDtypeStruct((m, d), F32),
        compiler_params=_cparams(("arbitrary",)),
        name="attn_combine_proj",
    )(os_[0], ls[0], os_[1], ls[1], os_[2], ls[2], sels[0], sels[1], w_all, x)


def _attn_sample_kernel(q_ref, kvn0_ref, kvn1_ref, kvn2_ref, c0_ref, c1_ref, c2_ref, o_ref, *, t_new):
    scale = HEAD_DIM ** -0.5
    caches = (c0_ref, c1_ref, c2_ref)
    news = (kvn0_ref, kvn1_ref, kvn2_ref)
    key_u = lax.broadcasted_iota(jnp.int32, (BAND, HEADS_A, 1), 0)

    for t in range(t_new):
        o_g, lse_g = [], []
        for g, (_, dil) in enumerate(DIL_PATTERNS):
            q = q_ref[g, 0, t]
            res = 0 if dil == 1 else t
            s_c = jnp.sum(caches[g][:, res, 0] * q[None], axis=-1, keepdims=True) * scale
            if dil == 1:
                s_c = jnp.where(key_u >= t, s_c, NEG)
                new_ts = list(range(t + 1))
            else:
                new_ts = [t]
            s_n = [jnp.sum(q * news[g][0, t2, 0], axis=-1, keepdims=True) * scale for t2 in new_ts]
            mx = functools.reduce(jnp.maximum, s_n, jnp.max(s_c, axis=0))
            p_c = jnp.exp(s_c - mx[None])
            den = jnp.sum(p_c, axis=0)
            acc = jnp.sum(p_c * caches[g][:, res, 1], axis=0)
            for t2, s in zip(new_ts, s_n):
                p_n = jnp.exp(s - mx)
                den = den + p_n
                acc = acc + p_n * news[g][0, t2, 1]
            o_g.append(acc / den)
            lse_g.append(mx + jnp.log(den))
        top = functools.reduce(jnp.maximum, lse_g)
        es = [jnp.exp(l - top) for l in lse_g]
        tot = _sum(es)
        o_ref[0, t] = _sum([(e / tot) * o for e, o in zip(es, o_g)])


def _attn_sample(q, kv_new, caches, layer, batch, t_new):
    tile = (HEADS_A, HEAD_DIM)
    in_specs = [pl.BlockSpec((N_GROUPS, 1, t_new) + tile, lambda b: (0, b, 0, 0, 0))]
    in_specs += [pl.BlockSpec((1, t_new, 2) + tile, lambda b: (b, 0, 0, 0, 0))] * N_GROUPS
    args = [q] + list(kv_new)
    for (win, dil), c in zip(DIL_PATTERNS, caches):
        n_res = min(dil, t_new)
        in_specs.append(pl.BlockSpec((None, None, BAND, n_res, 2) + tile,
                                     lambda b: (layer, b, 0, 0, 0, 0, 0)))
        args.append(c.reshape(c.shape[:2] + (BAND, dil, 2) + tile))
    kernel = functools.partial(_attn_sample_kernel, t_new=t_new)
    return pl.pallas_call(
        kernel,
        grid=(batch,),
        in_specs=in_specs,
        out_specs=pl.BlockSpec((1, t_new) + tile, lambda b: (b, 0, 0, 0)),
        out_shape=jax.ShapeDtypeStruct((batch, t_new) + tile, F32),
        compiler_params=_cparams(("arbitrary",)),
        name="attn_sample",
    )(*args)


def _matmul_residual_kernel(a_ref, w_ref, x_ref, o_ref):
    o_ref[...] = x_ref[...] + _dot(a_ref[...].astype(BF16), w_ref[...])


def _matmul_residual(a, w_all, layer, x, tm):
    m, k = a.shape
    n = w_all.shape[-1]
    tn = n
    return pl.pallas_call(
        _matmul_residual_kernel,
        grid=(m // tm, n // tn),
        in_specs=[
            pl.BlockSpec((tm, k), lambda i, j: (i, 0)),
            _layer_spec((k, tn), lambda i, j: (0, j), layer),
            pl.BlockSpec((tm, tn), lambda i, j: (i, j)),
        ],
        out_specs=pl.BlockSpec((tm, tn), lambda i, j: (i, j)),
        out_shape=jax.ShapeDtypeStruct((m, n), F32),
        compiler_params=_cparams(("arbitrary", "arbitrary")),
        name="matmul_residual",
    )(a, w_all, x)


def _gate_act(pre, is_input_gate):
    capped = GATE_CAP * jnp.tanh(pre / GATE_CAP)
    return jnp.where(is_input_gate, capped, _log_sigmoid(capped))


def _mlstm_in_kernel(x_ref, g_ref, w_ref, wg_ref, bg_ref, bgt_ref,
                     qkv_ref, og_ref, gates_ref, gatest_ref, h_scr, *, k_tiles, n_qkv_tiles, k_scale, n_gates):
    j = pl.program_id(1)

    @pl.when(j == 0)
    def _():
        h = _rms_rows(x_ref[...], g_ref[...]).astype(BF16)
        h_scr[...] = h
        pre = _dot_nt(h, wg_ref[...]) + bg_ref[...]
        lane = lax.broadcasted_iota(jnp.int32, pre.shape, 1)
        gates_ref[...] = jnp.where(lane < n_gates, _gate_act(pre, lane < n_gates // 2), 0.0)
        pre_t = _dot_nt(wg_ref[0:n_gates, :], h) + bgt_ref[...]
        row = lax.broadcasted_iota(jnp.int32, pre_t.shape, 0)
        gatest_ref[...] = _gate_act(pre_t, row < n_gates // 2)

    y = _dot_nt(h_scr[...], w_ref[...])

    @pl.when(j < n_qkv_tiles)
    def _():
        is_k = (j >= k_tiles[0]) & (j < k_tiles[1])
        qkv_ref[...] = (y * jnp.where(is_k, k_scale, 1.0).astype(F32)).astype(BF16)

    @pl.when(j >= n_qkv_tiles)
    def _():
        og_ref[...] = y


def _mlstm_in_proj(x, gains, layer, wt_all, ib, bg_all, bgt_all, tm, dqk, dv):
    m, d = x.shape
    tn = PROJ_COL_TILE
    qk_w, v_w = HEADS_B * dqk, HEADS_B * dv
    n_main = 2 * qk_w + 2 * v_w
    n_gates = wt_all.shape[1] - n_main
    n_qkv_tiles = (2 * qk_w + v_w) // tn
    nj = n_main // tn
    kernel = functools.partial(_mlstm_in_kernel, k_tiles=(qk_w // tn, 2 * qk_w // tn),
                               n_qkv_tiles=n_qkv_tiles, k_scale=dqk ** -0.5, n_gates=n_gates)
    return pl.pallas_call(
        kernel,
        grid=(m // tm, nj),
        in_specs=[
            pl.BlockSpec((tm, d), lambda i, j: (i, 0)),
            _layer_spec((1, d), lambda i, j: (0, 0), layer),
            _layer_spec((tn, d), lambda i, j: (j, 0), ib),
            _layer_spec((LANES, d), lambda i, j: (n_main // LANES, 0), ib),
            _layer_spec((1, LANES), lambda i, j: (0, 0), ib),
            _layer_spec((n_gates, 1), lambda i, j: (0, 0), ib),
        ],
        out_specs=[
            pl.BlockSpec((tm, tn), lambda i, j: (i, jnp.minimum(j, n_qkv_tiles - 1))),
            pl.BlockSpec((tm, tn), lambda i, j: (i, jnp.maximum(j - n_qkv_tiles, 0))),
            pl.BlockSpec((tm, LANES), lambda i, j: (i, 0)),
            pl.BlockSpec((n_gates, tm), lambda i, j: (0, i)),
        ],
        out_shape=[
            jax.ShapeDtypeStruct((m, 2 * qk_w + v_w), BF16),
            jax.ShapeDtypeStruct((m, v_w), F32),
            jax.ShapeDtypeStruct((m, LANES), F32),
            jax.ShapeDtypeStruct((n_gates, m), F32),
        ],
        scratch_shapes=[pltpu.VMEM((tm, d), BF16)],
        compiler_params=_cparams(("arbitrary", "arbitrary")),
        name="mlstm_in_proj",
    )(x, gains, wt_all, wt_all, bg_all, bgt_all)


def _mlstm_core_kernel(*refs, chunk, dqk, dv, has_state):
    q_ref, k_ref, v_ref, og_ref, gates_ref, gatest_ref, gh_ref = refs[:7]
    rest = refs[7:]
    if has_state:
        c0_ref, n0_ref, m0_ref = rest[:3]
        rest = rest[3:]
    out_ref, c_out, n_out, m_out, c_scr, n_scr, m_scr = rest
    c = pl.program_id(1)
    n_heads = c_scr.shape[0]

    @pl.when(c == 0)
    def _():
        if has_state:
            c_scr[...] = c0_ref[0]
            n_scr[...] = n0_ref[0]
            m_scr[...] = m0_ref[0]
        else:
            c_scr[...] = jnp.zeros_like(c_scr)
            n_scr[...] = jnp.zeros_like(n_scr)
            m_scr[...] = jnp.zeros_like(m_scr)

    row = lax.broadcasted_iota(jnp.int32, (chunk, chunk), 0)
    col = lax.broadcasted_iota(jnp.int32, (chunk, chunk), 1)
    causal = col <= row
    tri_lo = jnp.where(causal, 1.0, 0.0).astype(BF16)
    tri_up = jnp.where(row <= col, 1.0, 0.0).astype(BF16)

    gates = gates_ref[0]
    gates_t = gatest_ref[0] if len(gatest_ref.shape) == 3 else gatest_ref[...]
    b_cols = _sum([_dot(tri_lo, p) for p in _split3(gates)])
    b_rows = _sum([_dot(p, tri_up) for p in _split3(gates_t)])

    heads = range(n_heads)
    q = [q_ref[0, :, hh * dqk:(hh + 1) * dqk] for hh in heads]
    k = [k_ref[0, :, hh * dqk:(hh + 1) * dqk] for hh in heads]
    v = [v_ref[0, :, hh * dv:(hh + 1) * dv] for hh in heads]
    ig_col = [gates[:, hh:hh + 1] for hh in heads]
    b_col = [b_cols[:, n_heads + hh:n_heads + hh + 1] for hh in heads]
    m_prev = [m_scr[hh][:, 0:1] for hh in heads]

    inter, m_t, pexp = [], [], []
    for hh in heads:
        a_row = gates_t[hh:hh + 1, :] - b_rows[n_heads + hh:n_heads + hh + 1, :]
        a_mask = jnp.where(causal, a_row, NEG)
        a_max = jnp.max(_lane_fold(a_mask, jnp.maximum), axis=-1, keepdims=True)
        inter.append(b_col[hh] + m_prev[hh])
        m_t.append(jnp.maximum(inter[hh], b_col[hh] + a_max))
        pexp.append(jnp.exp(a_mask + (b_col[hh] - m_t[hh])))
    s = [_dot_nt(q[hh], k[hh]) for hh in heads]
    qc = [_dot_nt(q[hh], c_scr[hh].astype(BF16)) for hh in heads]
    p = [pexp[hh] * s[hh] for hh in heads]
    pv = [_dot(p[hh].astype(BF16), v[hh]) for hh in heads]
    for hh in heads:
        w_inter = jnp.exp(inter[hh] - m_t[hh])
        num = w_inter * qc[hh] + pv[hh]
        nq = jnp.sum(q[hh].astype(F32) * n_scr[hh], axis=-1, keepdims=True)
        den = w_inter * nq + jnp.sum(_lane_fold(p[hh], jnp.add), axis=-1, keepdims=True)
        r = 1.0 / jnp.maximum(jnp.abs(den), jnp.exp(-m_t[hh]))
        ms_num = jnp.sum(_lane_fold(num * num, jnp.add), axis=-1, keepdims=True) / dv
        scale = r * lax.rsqrt(r * r * ms_num + EPS)
        cols = slice(hh * dv, (hh + 1) * dv)
        gated = num * scale * gh_ref[:, cols] * jax.nn.sigmoid(og_ref[0, :, cols])
        out_ref[0, :, cols] = gated.astype(BF16)

    w_c, vw, w_s = [], [], []
    for hh in heads:
        b_last = b_col[hh][chunk - 1:chunk, :]
        dec = b_last - b_col[hh] + ig_col[hh]
        m_new = jnp.maximum(b_last + m_prev[hh], jnp.max(dec, axis=0, keepdims=True))
        w_c.append(jnp.exp(b_last + m_prev[hh] - m_new))
        w_s.append(jnp.exp(dec - m_new))
        vw.append((v[hh].astype(F32) * w_s[hh]).astype(BF16))
        m_scr[hh] = jnp.broadcast_to(m_new, (1, LANES))
    upd = [_dot_tn(vw[hh], k[hh]) for hh in heads]
    for hh in heads:
        c_scr[hh] = w_c[hh] * c_scr[hh] + upd[hh]
        n_scr[hh] = w_c[hh] * n_scr[hh] + jnp.sum(k[hh].astype(F32) * w_s[hh], axis=0, keepdims=True)

    @pl.when(c == pl.num_programs(1) - 1)
    def _():
        c_out[0] = c_scr[...]
        n_out[0] = n_scr[...]
        m_out[0] = m_scr[...]


def _mlstm_core(qkv, og, gates, gates_t, gh_all, ib, batch, seq, chunk, dqk, dv, state):
    n_heads = HEADS_B
    qk_w, v_w = n_heads * dqk, n_heads * dv
    nc = seq // chunk
    qkv3 = qkv.reshape(batch, seq, 2 * qk_w + v_w)
    state_specs = [
        pl.BlockSpec((1, n_heads, dv, dqk), lambda b, c: (b, 0, 0, 0)),
        pl.BlockSpec((1, n_heads, 1, dqk), lambda b, c: (b, 0, 0, 0)),
        pl.BlockSpec((1, n_heads, 1, LANES), lambda b, c: (b, 0, 0, 0)),
    ]
    in_specs = [
        pl.BlockSpec((1, chunk, qk_w), lambda b, c: (b, c, 0)),
        pl.BlockSpec((1, chunk, qk_w), lambda b, c: (b, c, 1)),
        pl.BlockSpec((1, chunk, v_w), lambda b, c: (b, c, 2 * qk_w // v_w)),
        pl.BlockSpec((1, chunk, v_w), lambda b, c: (b, c, 0)),
        pl.BlockSpec((1, chunk, LANES), lambda b, c: (b, c, 0)),
        pl.BlockSpec((2 * n_heads, chunk), lambda b, c: (0, b * nc + c)) if gates_t.ndim == 2
        else pl.BlockSpec((1, 2 * n_heads, chunk), lambda b, c: (b * nc + c, 0, 0)),
        _layer_spec((1, v_w), lambda b, c: (0, 0), ib),
    ]
    args = [qkv3, qkv3, qkv3, og.reshape(batch, seq, v_w), gates.reshape(batch, seq, LANES), gates_t, gh_all]
    if state is not None:
        in_specs += state_specs
        args += list(state)
    kernel = functools.partial(_mlstm_core_kernel, chunk=chunk, dqk=dqk, dv=dv,
                               has_state=state is not None)
    return pl.pallas_call(
        kernel,
        grid=(batch, nc),
        in_specs=in_specs,
        out_specs=[pl.BlockSpec((1, chunk, v_w), lambda b, c: (b, c, 0))] + state_specs,
        out_shape=[
            jax.ShapeDtypeStruct((batch, seq, v_w), BF16),
            jax.ShapeDtypeStruct((batch, n_heads, dv, dqk), F32),
            jax.ShapeDtypeStruct((batch, n_heads, 1, dqk), F32),
            jax.ShapeDtypeStruct((batch, n_heads, 1, LANES), F32),
        ],
        scratch_shapes=[pltpu.VMEM((n_heads, dv, dqk), F32), pltpu.VMEM((n_heads, 1, dqk), F32),
                        pltpu.VMEM((n_heads, 1, LANES), F32)],
        compiler_params=_cparams(("arbitrary", "arbitrary")),
        name="mlstm_core",
    )(*args)


def _ffn_kernel(*refs, rs, tiles_per_seq, t_new, cast_next):
    x_ref, g_ref, wa_ref, wb_ref, cwa_ref, cwb_ref, cba_ref, cbb_ref, wd_ref = refs[:9]
    rest = refs[9:]
    sample = t_new is not None
    if sample:
        s1a_ref, s1b_ref, s2a_ref, s2b_ref, o_ref, ua_ref, ub_ref, h_scr = rest
        s1_refs, s2_refs, u_refs = (s1a_ref, s1b_ref), (s2a_ref, s2b_ref), (ua_ref, ub_ref)
    else:
        if cast_next:
            up_next_ref, down_next_ref, o_ref, tail_ref, up_cast_ref, down_cast_ref, h_scr, tail_scr = rest
            up_cast_ref[...] = up_next_ref[...].astype(BF16)
            down_cast_ref[...] = down_next_ref[...].astype(BF16)
        else:
            o_ref, tail_ref, h_scr, tail_scr = rest
    i = pl.program_id(0)
    f = pl.program_id(1)
    tm = x_ref.shape[0]
    n_sub = tm // rs

    @pl.when(f == 0)
    def _():
        x = x_ref[...]
        h_scr[...] = _rms_rows(x, g_ref[...]).astype(BF16)
        o_ref[...] = x

    parts = ((wa_ref, cwa_ref, cba_ref), (wb_ref, cwb_ref, cbb_ref))
    if not sample:
        @pl.when((i == 0) & (f == 0))
        def _():
            tail_scr[...] = jnp.zeros_like(tail_scr)

        seq_start = i % tiles_per_seq == 0
        prev = []
        for part in range(2):
            t = jnp.where(seq_start, 0.0, tail_scr[f, part])
            prev.append((t[SUBLANES - 2:SUBLANES - 1, :], t[SUBLANES - 1:SUBLANES, :]))

    def up(r):
        h = h_scr[r * rs:(r + 1) * rs, :]
        return [_dot(h, w_ref[...]) for w_ref, _, _ in parts]

    def conv_gate(r, us):
        conv = []
        for part, (_, cw_ref, cb_ref) in enumerate(parts):
            u = us[part]
            ridx = lax.broadcasted_iota(jnp.int32, u.shape, 0)
            r1 = pltpu.roll(u, 1, 0)
            r2 = pltpu.roll(u, 2, 0)
            if sample:
                u_refs[part][...] = u
                pos = ridx % t_new
                u1 = jnp.where(pos >= 1, r1, s1_refs[part][...])
                u2 = jnp.where(pos >= 2, r2, s2_refs[part][...])
            else:
                p2, p1 = prev[part]
                u1 = jnp.where(ridx == 0, p1, r1)
                u2 = jnp.where(ridx == 0, p2, jnp.where(ridx == 1, p1, r2))
                prev[part] = (u[rs - 2:rs - 1, :], u[rs - 1:rs, :])
                if r == n_sub - 1:
                    tail = u[rs - SUBLANES:rs, :]
                    tail_scr[f, part] = tail
                    tail_ref[0, part] = tail
            cw = cw_ref[...]
            conv.append(cb_ref[...] + cw[0:1, :] * u2 + cw[1:2, :] * u1 + cw[2:3, :] * u)
        return (jax.nn.silu(conv[0]) * conv[1]).astype(BF16)

    us = up(0)
    for r in range(n_sub):
        us_next = up(r + 1) if r + 1 < n_sub else None
        z = conv_gate(r, us)
        o_ref[r * rs:(r + 1) * rs, :] += _dot(z, wd_ref[...])
        us = us_next


def _conv_ffn(x, gains, layer, w_up_all, conv_w_all, conv_b_all, w_down_all, w_layer, tm, rs,
              seq=None, sample_state=None, t_new=None, cast_next=None):
    m, d = x.shape
    d_ff = w_down_all.shape[1]
    tf = FFN_COL_TILE
    nf = d_ff // tf
    sample = sample_state is not None
    in_specs = [
        pl.BlockSpec((tm, d), lambda i, f: (i, 0)),
        _layer_spec((1, d), lambda i, f: (0, 0), layer),
        _layer_spec((d, tf), lambda i, f: (0, f), w_layer),
        _layer_spec((d, tf), lambda i, f: (0, nf + f), w_layer),
        _layer_spec((CONV_W, tf), lambda i, f: (0, f), layer),
        _layer_spec((CONV_W, tf), lambda i, f: (0, nf + f), layer),
        _layer_spec((1, tf), lambda i, f: (0, f), layer),
        _layer_spec((1, tf), lambda i, f: (0, nf + f), layer),
        _layer_spec((tf, d), lambda i, f: (f, 0), w_layer),
    ]
    args = [x, gains, w_up_all, w_up_all, conv_w_all, conv_w_all, conv_b_all, conv_b_all, w_down_all]
    scratch = [pltpu.VMEM((tm, d), BF16)]
    if sample:
        tiles_per_seq = None
        half_specs = [pl.BlockSpec((tm, tf), lambda i, f: (i, f)), pl.BlockSpec((tm, tf), lambda i, f: (i, nf + f))]
        in_specs += half_specs * 2
        args += [sample_state[0]] * 2 + [sample_state[1]] * 2
        out_specs = [pl.BlockSpec((tm, d), lambda i, f: (i, 0))] + [pl.BlockSpec((tm, tf), lambda i, f: (i, f))] * 2
        out_shape = [jax.ShapeDtypeStruct((m, d), F32)] + [jax.ShapeDtypeStruct((m, d_ff), F32)] * 2
    else:
        tiles_per_seq = seq // tm
        out_specs = [pl.BlockSpec((tm, d), lambda i, f: (i, 0)),
                     pl.BlockSpec((1, 2, SUBLANES, tf), lambda i, f: (i, 0, 0, f))]
        out_shape = [jax.ShapeDtypeStruct((m, d), F32),
                     jax.ShapeDtypeStruct((m // tm, 2, SUBLANES, d_ff), F32)]
        scratch.append(pltpu.VMEM((nf, 2, SUBLANES, tf), F32))
        if cast_next is not None:
            up_f32, down_f32, next_layer = cast_next
            n_steps = (m // tm) * nf
            cw, rw = 2 * d_ff // n_steps, d_ff // n_steps
            assert cw % LANES == 0 and rw % SUBLANES == 0 and cw * n_steps == 2 * d_ff and rw * n_steps == d_ff
            in_specs += [_layer_spec((d, cw), lambda i, f: (0, i * nf + f), next_layer),
                         _layer_spec((rw, d), lambda i, f: (i * nf + f, 0), next_layer)]
            args += [up_f32, down_f32]
            out_specs += [_layer_spec((d, cw), lambda i, f: (0, i * nf + f), 0),
                          _layer_spec((rw, d), lambda i, f: (i * nf + f, 0), 0)]
            out_shape += [jax.ShapeDtypeStruct((1, d, 2 * d_ff), BF16), jax.ShapeDtypeStruct((1, d_ff, d), BF16)]
    kernel = functools.partial(_ffn_kernel, rs=rs, tiles_per_seq=tiles_per_seq, t_new=t_new,
                               cast_next=cast_next is not None)
    return pl.pallas_call(
        kernel,
        grid=(m // tm, nf),
        in_specs=in_specs,
        out_specs=out_specs,
        out_shape=out_shape,
        scratch_shapes=scratch,
        compiler_params=_cparams(("arbitrary", "arbitrary")),
        name="conv_ffn_sample" if sample else "conv_ffn",
    )(*args)


def _shift_append(cache, new):
    n_buf, n_new = cache.shape[2], new.shape[2]
    zero = jnp.zeros((), cache.dtype)
    no_pad = (0, 0, 0)
    shifted = lax.pad(cache, zero, [no_pad, no_pad, (-n_new, n_new, 0)] + [no_pad] * (cache.ndim - 3))
    tail = lax.pad(new, zero, [no_pad, no_pad, (n_buf - n_new, 0, 0)] + [no_pad] * (cache.ndim - 3))
    row = lax.broadcasted_iota(jnp.int32, cache.shape, 2)
    return jnp.where(row < n_buf - n_new, shifted, tail)


def _rope_tables(pos):
    half = HEAD_DIM // 2
    inv_freq = ROPE_THETA ** (-jnp.arange(half, dtype=F32) / half)
    ang = pos.astype(F32)[:, None] * inv_freq[None, :]
    cos, sin = jnp.cos(ang), jnp.sin(ang)
    return jnp.concatenate([cos, cos], axis=-1), jnp.concatenate([-sin, sin], axis=-1)


def _perm_matrices(dil):
    pos = np.arange(PERM_TILE)
    slot = (pos % dil) * (PERM_TILE // dil) + pos // dil
    to_residue_major = np.zeros((PERM_TILE, PERM_TILE), np.float32)
    to_residue_major[slot, pos] = 1.0
    return jnp.asarray(to_residue_major, BF16), jnp.asarray(to_residue_major.T, BF16)


def kernel(x_prompt, x_sample, cache_kv_w128, cache_kv_w512, cache_kv_w2048, state_mlstm_C, state_mlstm_n, state_mlstm_m, state_ffn_conv, norm_mix, norm_ffn, attn_w_qkv, attn_q_norm, attn_k_norm, attn_w_o, mlstm_w_in, mlstm_b_gates, mlstm_norm_h, mlstm_w_out, ffn_w_up, ffn_conv_w, ffn_conv_b, ffn_w_down):
    batch, seq, d = x_prompt.shape
    dec_batch, t_new, _ = x_sample.shape
    depth = norm_mix.shape[0]
    caches = (cache_kv_w128, cache_kv_w512, cache_kv_w2048)
    dqk = state_mlstm_C.shape[-1]
    dv = state_mlstm_C.shape[-2]
    d_ff = ffn_w_down.shape[1]
    ms = dec_batch * t_new
    tm_p = ROW_TILE
    tile = (HEADS_A, HEAD_DIM)

    xp = x_prompt.reshape(batch * seq, d)
    xs = x_sample.reshape(ms, d)

    w_qkv, w_o, w_out = (a.astype(BF16) for a in (attn_w_qkv, attn_w_o, mlstm_w_out))
    w_up, w_down = ffn_w_up[:1].astype(BF16), ffn_w_down[:1].astype(BF16)
    g_mix = norm_mix[:, None, :]
    g_ffn = norm_ffn[:, None, :]
    conv_b = ffn_conv_b[:, None, :]
    n_main = 2 * HEADS_B * dqk + 2 * HEADS_B * dv
    n_gates = mlstm_b_gates.shape[-1]
    w_in_t = jnp.swapaxes(mlstm_w_in, 1, 2).astype(BF16)
    bg_row = jnp.zeros((mlstm_b_gates.shape[0], 1, LANES), F32).at[:, 0, :n_gates].set(mlstm_b_gates)
    bg_col = mlstm_b_gates[:, :, None]
    gh = mlstm_norm_h[:, None, :]

    cos_p, sin_p = _rope_tables(jnp.arange(seq))
    cos_s, sin_s = _rope_tables(PAST_LEN + jnp.arange(t_new))
    cos_s, sin_s = jnp.tile(cos_s, (dec_batch, 1)), jnp.tile(sin_s, (dec_batch, 1))
    perms = [None] + [_perm_matrices(dil) for _, dil in DIL_PATTERNS[1:]]

    kv_p = [None] * N_GROUPS
    n_attn_layers = (depth + 1) // 2
    kv_new = [[] for _ in DIL_PATTERNS]
    c_p, n_p, m_p, c_s, n_s, m_s = [], [], [], [], [], []
    conv_p, conv_s = [], []

    for layer in range(depth):
        if layer % 2 == 0:
            ia = layer // 2
            gains = jnp.stack([attn_q_norm[ia], attn_k_norm[ia]])
            h = xp
            os_, ls = [], []
            for g, (win, dil) in enumerate(DIL_PATTERNS):
                perm = None if perms[g] is None else perms[g][0]
                q, k, v, kv_p[g], *h_out = _group_proj(h, w_qkv, ia, g, gains, cos_p, sin_p, perm,
                                                       ATTN_ROW_TILE, ATTN_ROW_TILE,
                                                       kv_layers=n_attn_layers, kv_stack=kv_p[g],
                                                       norm=(g_mix, layer) if g == 0 else None)
                if h_out:
                    h = h_out[0]
                o_g, l_g = _attn_prompt(q, k, v, g, batch, seq)
                os_.append(o_g)
                ls.append(l_g)
            xp = _combine_proj(os_, ls, [perms[1][1], perms[2][1]], w_o, ia, xp, ATTN_ROW_TILE, ROW_SUBTILE)
            h = _norm(xs, g_mix, layer, ms)
            qs, news = [], []
            for g in range(N_GROUPS):
                q, _, _, kvf = _group_proj(h, w_qkv, ia, g, gains, cos_s, sin_s, None, ms, ms)
                qs.append(q.astype(F32).reshape((dec_batch, t_new) + tile))
                news.append(kvf.reshape((dec_batch, t_new, 2) + tile))
                kv_new[g].append(news[-1])
            comb_s = _attn_sample(jnp.stack(qs), news, caches, ia, dec_batch, t_new)
            xs = _matmul_residual(comb_s.reshape(ms, ATT_W), w_o, ia, xs, ms)
        else:
            ib = layer // 2
            qkv, og, gates, gates_t = _mlstm_in_proj(xp, g_mix, layer, w_in_t, ib, bg_row, bg_col,
                                                     tm_p, dqk, dv)
            gated, c_f, n_f, m_f = _mlstm_core(qkv, og, gates, gates_t, gh, ib, batch, seq,
                                               MLSTM_CHUNK, dqk, dv, None)
            xp = _matmul_residual(gated.reshape(batch * seq, HEADS_B * dv), w_out, ib, xp, tm_p)
            c_p.append(c_f)
            n_p.append(n_f[:, :, 0, :])
            m_p.append(m_f[:, :, 0, 0])
            qkv, og, gates, _ = _mlstm_in_proj(xs, g_mix, layer, w_in_t, ib, bg_row, bg_col, ms, dqk, dv)
            tp = SAMPLE_PAD_T
            n_pad = tp - t_new

            def front_pad(a):
                a = a.reshape(dec_batch, t_new, a.shape[-1])
                a = jnp.pad(a, ((0, 0), (n_pad, 0), (0, 0)))
                return a.reshape(dec_batch * tp, a.shape[-1])

            lane_is_ig = (jnp.arange(LANES) < HEADS_B)[None, None, :]
            pad_gates = jnp.broadcast_to(jnp.where(lane_is_ig, NEG, 0.0).astype(F32), (dec_batch, n_pad, LANES))
            g3 = jnp.concatenate([pad_gates, gates.reshape(dec_batch, t_new, LANES)], axis=1)
            gates_t_pad = jnp.swapaxes(g3[:, :, :n_gates], 1, 2)
            state = (state_mlstm_C[ib], state_mlstm_n[ib][:, :, None, :],
                     jnp.broadcast_to(state_mlstm_m[ib][:, :, None, None], (dec_batch, HEADS_B, 1, LANES)))
            gated, c_f, n_f, m_f = _mlstm_core(front_pad(qkv), front_pad(og), g3.reshape(dec_batch * tp, LANES),
                                               gates_t_pad, gh, ib, dec_batch, tp, tp, dqk, dv, state)
            gated = gated[:, n_pad:].reshape(ms, HEADS_B * dv)
            xs = _matmul_residual(gated, w_out, ib, xs, ms)
            c_s.append(c_f)
            n_s.append(n_f[:, :, 0, :])
            m_s.append(m_f[:, :, 0, 0])

        cast_next = (ffn_w_up, ffn_w_down, layer + 1) if layer + 1 < depth else None
        xp, tails, *w_next = _conv_ffn(xp, g_ffn, layer, w_up, ffn_conv_w, conv_b, w_down, 0, tm_p,
                                       FFN_ROW_SUBTILE, seq=seq, cast_next=cast_next)
        tiles_per_seq = seq // tm_p
        tails = tails[tiles_per_seq - 1::tiles_per_seq, :, SUBLANES - (CONV_W - 1):, :]
        conv_p.append(jnp.swapaxes(tails, 1, 2).reshape(batch, CONV_W - 1, 2 * d_ff))
        st = state_ffn_conv[layer]
        pad_rows = t_new - (CONV_W - 1)
        s2 = jnp.pad(st, ((0, 0), (0, pad_rows), (0, 0))).reshape(ms, 2 * d_ff)
        s1 = jnp.pad(st[:, 1:], ((0, 0), (0, pad_rows + 1), (0, 0))).reshape(ms, 2 * d_ff)
        xs, u_a, u_b = _conv_ffn(xs, g_ffn, layer, w_up, ffn_conv_w, conv_b, w_down, 0, ms, ms,
                                 sample_state=(s1, s2), t_new=t_new)
        if w_next:
            w_up, w_down = w_next
        u_s = jnp.concatenate([u_a, u_b], axis=-1).reshape(dec_batch, t_new, 2 * d_ff)
        ext = jnp.concatenate([state_ffn_conv[layer], u_s], axis=1)
        conv_s.append(ext[:, t_new:])

    stack = jnp.stack
    kv_s = [_shift_append(c, stack(new)) for c, new in zip(caches, kv_new)]
    kv_p = [a.reshape((n_attn_layers, batch, seq, 2) + tile)[:, :, seq - min(win, seq):]
            for a, (win, _) in zip(kv_p, DIL_PATTERNS)]
    return (xp.reshape(batch, seq, d), xs.reshape(dec_batch, t_new, d),
            kv_p[0], kv_s[0], kv_p[1], kv_s[1], kv_p[2], kv_s[2],
            stack(c_p), stack(c_s), stack(n_p), stack(n_s), stack(m_p), stack(m_s),
            stack(conv_p), stack(conv_s))
```
